```python
import math
import jax, jax.numpy as jnp
from jax import lax
import numpy as np

D_MODEL = 1024
BATCH = 8
SEQ = 4096
DEPTH = 1

MLA_HEADS = 8
QK_NOPE = 64
QK_ROPE = 32
V_HEAD = 64
Q_LORA = 256
KV_LORA = 128
ROPE_THETA = 10000.0
Q_BLOCK = 128
MLA_WIDTH = MLA_HEADS * V_HEAD
RWKV_HEADS = 8
RWKV_HEAD = 64
RWKV_WIDTH = RWKV_HEADS * RWKV_HEAD
DECAY_LORA = 64
A_LORA = 64
GATE_LORA = 128
GN_EPS = 64e-5
N_BRANCHES = 2
MIX_WIDTH = MLA_WIDTH + RWKV_WIDTH
FFN_HIDDEN = 2816
CONV_WIDTH = 3
NORM_EPS = 1e-6
MLA_COLS = Q_LORA + KV_LORA + QK_ROPE
RWKV_COLS = 3 * RWKV_WIDTH + A_LORA + 2 * DECAY_LORA + 2 * GATE_LORA
GATE_COLS = N_BRANCHES * D_MODEL
IN_COLS = MLA_COLS + RWKV_COLS + GATE_COLS

kernel_name = "hybrid_mla_rwkv7_gated_encoder"


def _split(t, sizes):
    out, start = [], 0
    for s in sizes:
        out.append(t[..., start:start + s])
        start += s
    return out


def rms_norm(t, g, eps=NORM_EPS):
    tf = t.astype(jnp.float32)
    y = tf * lax.rsqrt(jnp.mean(tf * tf, axis=-1, keepdims=True) + eps)
    return (y * g.astype(jnp.float32)).astype(t.dtype)


def rope(t, cos, sin):
    half = t.shape[-1] // 2
    t1, t2 = t[..., :half], t[..., half:]
    return jnp.concatenate([t1 * cos - t2 * sin, t2 * cos + t1 * sin], axis=-1)


def centered_shift(u, mu_prev, mu_next):
    zero = jnp.zeros_like(u[:, :1])
    prev = jnp.concatenate([zero, u[:, :-1]], axis=1)
    nxt = jnp.concatenate([u[:, 1:], zero], axis=1)
    return u + mu_prev * (prev - u) + mu_next * (nxt - u)


def dwconv_centered(u, w, b):
    up = jnp.pad(u, ((0, 0), (1, 1), (0, 0)))
    return up[:, :-2] * w[0] + up[:, 1:-1] * w[1] + up[:, 2:] * w[2] + b


def blocked_mla_attention(q_nope, q_rope, k_nope, k_rope, v):
    B, S, H, _ = q_nope.shape
    nb = S // Q_BLOCK
    scale = 1.0 / math.sqrt(QK_NOPE + QK_ROPE)
    qn = jnp.moveaxis(q_nope.reshape(B, nb, Q_BLOCK, H, QK_NOPE), 1, 0)
    qr = jnp.moveaxis(q_rope.reshape(B, nb, Q_BLOCK, H, QK_ROPE), 1, 0)

    def one_block(args):
        qn_b, qr_b = args
        s = (jnp.einsum('bqhd,bkhd->bhqk', qn_b, k_nope)
             + jnp.einsum('bqhd,bkd->bhqk', qr_b, k_rope))
        p = jax.nn.softmax(s.astype(jnp.float32) * scale, axis=-1)
        return jnp.einsum('bhqk,bkhd->bqhd', p.astype(v.dtype), v)

    o = lax.map(one_block, (qn, qr))
    return jnp.moveaxis(o, 0, 1).reshape(B, S, H * V_HEAD)


def mla_branch(cols, cos, sin, q_a_norm_g, kv_a_norm_g, w_uq, w_ukv, qn_g, qr_g, kn_g, kr_g):
    B, S, _ = cols.shape
    c_q, c_kv, k_rope = _split(cols, [Q_LORA, KV_LORA, QK_ROPE])
    c_q = rms_norm(c_q, q_a_norm_g)
    c_kv = rms_norm(c_kv, kv_a_norm_g)
    q = (c_q @ w_uq).reshape(B, S, MLA_HEADS, QK_NOPE + QK_ROPE)
    kv = (c_kv @ w_ukv).reshape(B, S, MLA_HEADS, QK_NOPE + V_HEAD)
    q_nope, q_rope = q[..., :QK_NOPE], q[..., QK_NOPE:]
    k_nope, v = kv[..., :QK_NOPE], kv[..., QK_NOPE:]
    q_nope = rms_norm(q_nope, qn_g)
    k_nope = rms_norm(k_nope, kn_g)
    q_rope = rope(rms_norm(q_rope, qr_g), cos[:, :, None, :], sin[:, :, None, :])
    k_rope = rope(rms_norm(k_rope, kr_g), cos, sin)
    return blocked_mla_attention(q_nope, q_rope, k_nope, k_rope, v)


def wkv7_scan(r, w, k, v, a, b, reverse):
    B, S, H, N = r.shape
    xs = tuple(jnp.moveaxis(t.astype(jnp.float32), 1, 0) for t in (r, w, k, v, a, b))

    def step(state, inp):
        r_t, w_t, k_t, v_t, a_t, b_t = inp
        sa = jnp.einsum('bhvk,bhk->bhv', state, a_t)
        state = (state * w_t[:, :, None, :] + sa[..., None] * b_t[:, :, None, :]
                 + v_t[..., None] * k_t[:, :, None, :])
        return state, jnp.einsum('bhvk,bhk->bhv', state, r_t)

    s0 = jnp.zeros((B, H, N, N), jnp.float32)
    _, out = lax.scan(step, s0, xs, reverse=reverse)
    return jnp.moveaxis(out, 0, 1)


def head_group_norm(o, g, b):
    mu = jnp.mean(o, axis=-1, keepdims=True)
    var = jnp.mean(jnp.square(o - mu), axis=-1, keepdims=True)
    y = (o - mu) * lax.rsqrt(var + GN_EPS)
    return (y * g.astype(jnp.float32).reshape(RWKV_HEADS, RWKV_HEAD)
            + b.astype(jnp.float32).reshape(RWKV_HEADS, RWKV_HEAD))


def rwkv7_branch(cols, shift_mu, w0, w2, a0, a2, g2, k_k, k_a, r_k, ln_g, ln_b):
    B, S, _ = cols.shape
    u = centered_shift(cols, shift_mu[0], shift_mu[1])
    r, k, v, a_lo, dlo_f, dlo_b, glo_f, glo_b = _split(
        u, [RWKV_WIDTH, RWKV_WIDTH, RWKV_WIDTH, A_LORA, DECAY_LORA, DECAY_LORA, GATE_LORA, GATE_LORA])
    heads = lambda t: t.reshape(B, S, RWKV_HEADS, RWKV_HEAD)
    a = jax.nn.sigmoid(a0 + a_lo @ a2)
    kkf = heads(k * k_k).astype(jnp.float32)
    kk = kkf / jnp.maximum(jnp.sqrt(jnp.sum(kkf * kkf, axis=-1, keepdims=True)), 1e-12)
    k = k * (1.0 + (a - 1.0) * k_a)
    rh, kh, vh, ah = heads(r), heads(k), heads(v), heads(a)
    bonus = (jnp.sum(rh * kh * r_k, axis=-1, keepdims=True) * vh).astype(jnp.float32)
    b_vec = kk * ah.astype(jnp.float32)

    def direction(dlo, glo, w0_d, w2_d, g2_d, reverse):
        w = -jax.nn.softplus(-(w0_d + jnp.tanh(dlo) @ w2_d)) - 0.5
        decay = jnp.exp(-jnp.exp(w.astype(jnp.float32)))
        o = wkv7_scan(rh, heads(decay), kh, vh, -kk, b_vec, reverse)
        o = (head_group_norm(o, ln_g, ln_b) + bonus).reshape(B, S, RWKV_WIDTH)
        g = jax.nn.sigmoid(glo) @ g2_d
        return o.astype(cols.dtype) * g

    fwd = direction(dlo_f, glo_f, w0[0], w2[0], g2[0], False)
    bwd = direction(dlo_b, glo_b, w0[1], w2[1], g2[1], True)
    return fwd + bwd


def setup_inputs(seed: int = 0) -> dict:
    key = jax.random.key(seed)
    ks = iter(jax.random.split(key, 40))
    nrm = lambda shape, s: jax.random.normal(next(ks), shape, jnp.float32) * s
    gain = lambda shape: 1.0 + nrm(shape, 0.05)
    L, D = DEPTH, D_MODEL
    x = jax.random.normal(next(ks), (BATCH, SEQ, D), jnp.float32)
    offs = jax.random.randint(next(ks), (BATCH, 1), 0, 2048, dtype=jnp.int32)
    positions = offs + jnp.arange(SEQ, dtype=jnp.int32)[None, :]
    return {
        "x": x,
        "positions": positions,
        "norm_mix_g": gain((L, D)),
        "w_in": nrm((L, D, IN_COLS), D ** -0.5),
        "b_gate": nrm((L, N_BRANCHES, D), 0.1),
        "q_a_norm_g": gain((L, Q_LORA)),
        "kv_a_norm_g": gain((L, KV_LORA)),
        "w_uq": nrm((L, Q_LORA, MLA_HEADS * (QK_NOPE + QK_ROPE)), Q_LORA ** -0.5),
        "w_ukv": nrm((L, KV_LORA, MLA_HEADS * (QK_NOPE + V_HEAD)), KV_LORA ** -0.5),
        "qn_norm_g": gain((L, QK_NOPE)),
        "qr_norm_g": gain((L, QK_ROPE)),
        "kn_norm_g": gain((L, QK_NOPE)),
        "kr_norm_g": gain((L, QK_ROPE)),
        "shift_mu": jax.random.uniform(next(ks), (L, 2, RWKV_COLS), jnp.float32, 0.0, 0.5),
        "w0": jax.random.uniform(next(ks), (L, 2, RWKV_WIDTH), jnp.float32, -6.0, -1.0),
        "w2": nrm((L, 2, DECAY_LORA, RWKV_WIDTH), 0.1 * DECAY_LORA ** -0.5),
        "a0": nrm((L, RWKV_WIDTH), 0.1),
        "a2": nrm((L, A_LORA, RWKV_WIDTH), 0.5 * A_LORA ** -0.5),
        "g2": nrm((L, 2, GATE_LORA, RWKV_WIDTH), GATE_LORA ** -0.5),
        "k_k": 0.85 + nrm((L, RWKV_WIDTH), 0.05),
        "k_a": gain((L, RWKV_WIDTH)),
        "r_k": nrm((L, RWKV_HEADS, RWKV_HEAD), 0.1),
        "ln_x_g": gain((L, RWKV_WIDTH)),
        "ln_x_b": nrm((L, RWKV_WIDTH), 0.02),
        "w_o": nrm((L, MIX_WIDTH, D), MLA_WIDTH ** -0.5),
        "w_merge": nrm((L, D, D), D ** -0.5),
        "norm_ffn_g": gain((L, D)),
        "w_ffn_gate": nrm((L, D, FFN_HIDDEN), D ** -0.5),
        "w_ffn_up": nrm((L, D, FFN_HIDDEN), D ** -0.5),
        "ffn_conv_w": nrm((L, CONV_WIDTH, FFN_HIDDEN), CONV_WIDTH ** -0.5),
        "ffn_conv_b": nrm((L, FFN_HIDDEN), 0.02),
        "w_ffn_down": nrm((L, FFN_HIDDEN, D), FFN_HIDDEN ** -0.5),
    }


def reference(x, positions, norm_mix_g, w_in, b_gate, q_a_norm_g, kv_a_norm_g, w_uq, w_ukv,
              qn_norm_g, qr_norm_g, kn_norm_g, kr_norm_g, shift_mu, w0, w2, a0, a2, g2,
              k_k, k_a, r_k, ln_x_g, ln_x_b, w_o, w_merge, norm_ffn_g, w_ffn_gate, w_ffn_up,
              ffn_conv_w, ffn_conv_b, w_ffn_down):
    B, S, D = x.shape
    inv_freq = ROPE_THETA ** (-jnp.arange(0, QK_ROPE, 2, dtype=jnp.float32) / QK_ROPE)
    ang = positions.astype(jnp.float32)[..., None] * inv_freq
    cos, sin = jnp.cos(ang).astype(x.dtype), jnp.sin(ang).astype(x.dtype)

    for l in range(DEPTH):
        h = rms_norm(x, norm_mix_g[l])
        proj = h @ w_in[l]
        mla_cols, rw_cols, gate_cols = _split(proj, [MLA_COLS, RWKV_COLS, GATE_COLS])
        o_a = mla_branch(mla_cols, cos, sin, q_a_norm_g[l], kv_a_norm_g[l], w_uq[l], w_ukv[l],
                         qn_norm_g[l], qr_norm_g[l], kn_norm_g[l], kr_norm_g[l])
        o_b = rwkv7_branch(rw_cols, shift_mu[l], w0[l], w2[l], a0[l], a2[l], g2[l],
                           k_k[l], k_a[l], r_k[l], ln_x_g[l], ln_x_b[l])
        w_o_l = w_o[l]
        y_a = o_a @ w_o_l[:MLA_WIDTH]
        y_b = o_b @ w_o_l[MLA_WIDTH:]
        gates = jax.nn.sigmoid(gate_cols.reshape(B, S, N_BRANCHES, D) + b_gate[l])
        x = x + (gates[:, :, 0] * y_a + gates[:, :, 1] * y_b) @ w_merge[l]
        h = rms_norm(x, norm_ffn_g[l])
        gp = dwconv_centered(h @ w_ffn_gate[l], ffn_conv_w[l], ffn_conv_b[l])
        x = x + (jax.nn.silu(gp) * (h @ w_ffn_up[l])) @ w_ffn_down[l]
    return x
```

```python
import functools
import math

import jax
import jax.numpy as jnp
from jax import lax
from jax.experimental import pallas as pl
from jax.experimental.pallas import tpu as pltpu

F32 = jnp.float32
BF16 = jnp.bfloat16

D_MODEL = 1024
MLA_HEADS = 8
QK_NOPE = 64
QK_ROPE = 32
V_HEAD = 64
Q_LORA = 256
KV_LORA = 128
ROPE_THETA = 10000.0
RWKV_HEADS = 8
RWKV_HEAD = 64
RWKV_WIDTH = RWKV_HEADS * RWKV_HEAD
DECAY_LORA = 64
A_LORA = 64
GATE_LORA = 128
GN_EPS = 64e-5
FFN_HIDDEN = 2816
NORM_EPS = 1e-6

LANE = 128
MLA_W = 512
RW_W = 3 * RWKV_WIDTH + 5 * LANE
VMEM_LIMIT = 56 * 1024 * 1024

CHUNK = 64
GROUP = 4
GW = GROUP * RWKV_HEAD


def _mm(a, b):
    return jnp.dot(a, b, preferred_element_type=F32)


def _mm_nt(a, b):
    return lax.dot_general(a, b, (((1,), (1,)), ((), ())), preferred_element_type=F32)


def _mm_tn(a, b):
    return lax.dot_general(a, b, (((0,), (0,)), ((), ())), preferred_element_type=F32)


def _split(x):
    hi = x.astype(BF16)
    lo = (x - hi.astype(F32)).astype(BF16)
    return hi, lo


def _mm_hl(x, b01):
    hi, lo = _split(x)
    return _mm(hi, b01) + _mm(lo, b01)


def _sigmoid(x):
    return 1.0 / (1.0 + jnp.exp(-x))


def _rms(x, g, eps=NORM_EPS):
    return x * lax.rsqrt(jnp.mean(x * x, axis=-1, keepdims=True) + eps) * g


def _const_spec(shape):
    nd = len(shape)
    return pl.BlockSpec(shape, lambda *_: (0,) * nd, pipeline_mode=pl.Buffered(1))


def _params(sem):
    return pltpu.CompilerParams(dimension_semantics=sem, vmem_limit_bytes=VMEM_LIMIT)


def _inproj_kernel(x_ref, g_ref, wm_ref, wr_ref, wg_ref, bg_ref, mla_ref, rw_ref, gate_ref):
    h = _rms(x_ref[...], g_ref[...]).astype(BF16)
    mla_ref[...] = _mm(h, wm_ref[...])
    rw_ref[...] = _mm(h, wr_ref[...])
    gate_ref[...] = _sigmoid(_mm(h, wg_ref[...]) + bg_ref[...])


def _inproj(x2, g, wm, wr, wg, bg, tm):
    T = x2.shape[0]
    row = lambda w: pl.BlockSpec((tm, w), lambda i: (i, 0))
    return pl.pallas_call(
        _inproj_kernel,
        grid=(T // tm,),
        in_specs=[row(D_MODEL), _const_spec(g.shape), _const_spec(wm.shape), _const_spec(wr.shape),
                  _const_spec(wg.shape), _const_spec(bg.shape)],
        out_specs=[row(MLA_W), row(RW_W), row(2 * D_MODEL)],
        out_shape=[jax.ShapeDtypeStruct((T, MLA_W), F32), jax.ShapeDtypeStruct((T, RW_W), F32),
                   jax.ShapeDtypeStruct((T, 2 * D_MODEL), F32)],
        compiler_params=_params(("parallel",)),
        name="inproj",
    )(x2, g, wm, wr, wg, bg)


def _rotary(t, cos, sin, lane):
    first = (lane >= QK_NOPE) & (lane < QK_NOPE + QK_ROPE // 2)
    second = (lane >= QK_NOPE + QK_ROPE // 2) & (lane < QK_NOPE + QK_ROPE)
    up = pltpu.roll(t, LANE - QK_ROPE // 2, axis=1)
    dn = pltpu.roll(t, QK_ROPE // 2, axis=1)
    rot = jnp.where(first, -up, jnp.where(second, dn, 0.0))
    return t * cos + rot * sin


def _mla_prep_kernel(mla_ref, pos_ref, gq_ref, gkv_ref, wq_ref, wk_ref, wvt_ref, gqh_ref, gkn_ref,
                     gkr_ref, freq_ref, q_ref, k_ref, vt_ref):
    tm = mla_ref.shape[0]
    lane = lax.broadcasted_iota(jnp.int32, (tm, LANE), 1)
    nope = lane < QK_NOPE
    ang = pos_ref[...].astype(F32) * freq_ref[...]
    cos, sin = jnp.cos(ang), jnp.sin(ang)
    cq = _rms(mla_ref[:, 0:Q_LORA], gq_ref[...]).astype(BF16)
    ckv = _rms(mla_ref[:, Q_LORA:Q_LORA + KV_LORA], gkv_ref[...]).astype(BF16)
    kr = mla_ref[:, Q_LORA + KV_LORA:MLA_W]
    kr = kr * lax.rsqrt(jnp.sum(kr * kr, axis=-1, keepdims=True) / QK_ROPE + NORM_EPS) * gkr_ref[...]
    kr = _rotary(kr, cos, sin, lane)
    scale = 1.0 / math.sqrt(QK_NOPE + QK_ROPE)
    for h in range(MLA_HEADS):
        q = _mm(cq, wq_ref[h])
        q2 = q * q
        ss_n = jnp.sum(jnp.where(nope, q2, 0.0), axis=-1, keepdims=True)
        ss_r = jnp.sum(jnp.where(nope, 0.0, q2), axis=-1, keepdims=True)
        inv = jnp.where(nope, lax.rsqrt(ss_n / QK_NOPE + NORM_EPS), lax.rsqrt(ss_r / QK_ROPE + NORM_EPS))
        q = _rotary(q * inv * gqh_ref[...], cos, sin, lane)
        q_ref[0, h] = (q * scale).astype(BF16)
        kn = _mm(ckv, wk_ref[h])
        kn = kn * lax.rsqrt(jnp.sum(kn * kn, axis=-1, keepdims=True) / QK_NOPE + NORM_EPS) * gkn_ref[...]
        k_ref[0, h] = (kn + kr).astype(BF16)
    vt_ref[0] = _mm_nt(wvt_ref[...], ckv).astype(BF16)


def _mla_prep(mla, pos, gq, gkv, wq, wk, wvt, gqh, gkn, gkr, freq, B, S, tm):
    nj = S // tm
    H = MLA_HEADS
    consts = [gq, gkv, wq, wk, wvt, gqh, gkn, gkr, freq]
    return pl.pallas_call(
        _mla_prep_kernel,
        grid=(B, nj),
        in_specs=[pl.BlockSpec((tm, MLA_W), lambda b, j: (b * nj + j, 0)),
                  pl.BlockSpec((tm, 1), lambda b, j: (b * nj + j, 0))] + [_const_spec(c.shape) for c in consts],
        out_specs=[pl.BlockSpec((1, H, tm, LANE), lambda b, j: (b, 0, j, 0)),
                   pl.BlockSpec((1, H, tm, LANE), lambda b, j: (b, 0, j, 0)),
                   pl.BlockSpec((1, H * V_HEAD, tm), lambda b, j: (b, 0, j))],
        out_shape=[jax.ShapeDtypeStruct((B, H, S, LANE), BF16), jax.ShapeDtypeStruct((B, H, S, LANE), BF16),
                   jax.ShapeDtypeStruct((B, H * V_HEAD, S), BF16)],
        compiler_params=_params(("parallel", "parallel")),
        name="mla_prep",
    )(mla, pos, *consts)


def _attn_kernel(q_ref, k_ref, vt_ref, ot_ref):
    s = _mm_nt(k_ref[0, 0], q_ref[0, 0])
    m = jnp.max(s, axis=0, keepdims=True)
    p = jnp.exp(s - m)
    l = jnp.sum(p, axis=0, keepdims=True)
    o = _mm(vt_ref[0], p.astype(BF16))
    ot_ref[0] = (o / l).astype(BF16)


def _attn(q, k, vt, tq):
    B, H, S, _ = q.shape
    return pl.pallas_call(
        _attn_kernel,
        grid=(B, H, S // tq),
        in_specs=[pl.BlockSpec((1, 1, tq, LANE), lambda b, h, i: (b, h, i, 0)),
                  pl.BlockSpec((1, 1, S, LANE), lambda b, h, i: (b, h, 0, 0)),
                  pl.BlockSpec((1, V_HEAD, S), lambda b, h, i: (b, h, 0))],
        out_specs=pl.BlockSpec((1, V_HEAD, tq), lambda b, h, i: (b, h, i)),
        out_shape=jax.ShapeDtypeStruct((B, H * V_HEAD, S), BF16),
        compiler_params=_params(("parallel", "parallel", "parallel")),
        name="attn",
    )(q, k, vt)


def _shifted(p, halo, tm):
    row = lax.broadcasted_iota(jnp.int32, p.shape, 0)
    prev = jnp.where(row == 0, halo[0:1, :], pltpu.roll(p, 1, axis=0))
    nxt = jnp.where(row == tm - 1, halo[1:2, :], pltpu.roll(p, tm - 1, axis=0))
    return prev, nxt


def _rwkv_prep_kernel(rw_ref, halo_ref, mu_ref, a0_ref, a2_ref, w0_ref, w2_ref, g2_ref, kk_ref, ka_ref,
                      rk_ref, bo_ref, r_out, k_out, v_out, kk_out, bv_out, lw_out, bonus_out, g_out):
    tm = rw_ref.shape[0]
    p = rw_ref[...]
    prev, nxt = _shifted(p, halo_ref[0], tm)
    u = p + mu_ref[0:1, :] * (prev - p) + mu_ref[1:2, :] * (nxt - p)
    W = RWKV_WIDTH
    r, k, v = u[:, 0:W], u[:, W:2 * W], u[:, 2 * W:3 * W]
    a_lo = u[:, 3 * W:3 * W + LANE]
    a = _sigmoid(a0_ref[...] + _mm(a_lo.astype(BF16), a2_ref[...]))
    bo = bo_ref[...]
    kkf = k * kk_ref[...]
    kkn = kkf / jnp.maximum(jnp.sqrt(_mm_hl(kkf * kkf, bo)), 1e-12)
    k = k * (1.0 + (a - 1.0) * ka_ref[...])
    r_out[...] = r
    k_out[...] = k
    v_out[...] = v
    kk_out[...] = kkn
    bv_out[...] = kkn * a
    bonus_out[...] = _mm_hl(r * k * rk_ref[...], bo) * v
    for d in range(2):
        dlo = u[:, 3 * W + (1 + d) * LANE:3 * W + (2 + d) * LANE]
        glo = u[:, 3 * W + (3 + d) * LANE:3 * W + (4 + d) * LANE]
        z = -(w0_ref[d:d + 1, :] + _mm(jnp.tanh(dlo).astype(BF16), w2_ref[d]))
        softplus = jnp.maximum(z, 0.0) + jnp.log(1.0 + jnp.exp(-jnp.abs(z)))
        lw_out[d] = -jnp.exp(-softplus - 0.5)
        g_out[d] = _mm(_sigmoid(glo).astype(BF16), g2_ref[d])


def _rwkv_prep(rw, halo, mu, a0, a2, w0, w2, g2, k_k, k_a, r_k, bo, tm):
    T = rw.shape[0]
    W = RWKV_WIDTH
    consts = [mu, a0, a2, w0, w2, g2, k_k, k_a, r_k, bo]
    row = pl.BlockSpec((tm, W), lambda i: (i, 0))
    row2 = pl.BlockSpec((2, tm, W), lambda i: (0, i, 0))
    one = jax.ShapeDtypeStruct((T, W), F32)
    two = jax.ShapeDtypeStruct((2, T, W), F32)
    return pl.pallas_call(
        _rwkv_prep_kernel,
        grid=(T // tm,),
        in_specs=[pl.BlockSpec((tm, RW_W), lambda i: (i, 0)), pl.BlockSpec((1, 8, RW_W), lambda i: (i, 0, 0))]
        + [_const_spec(c.shape) for c in consts],
        out_specs=[row, row, row, row, row, row2, row, row2],
        out_shape=[one, one, one, one, one, two, one, two],
        compiler_params=_params(("parallel",)),
        name="rwkv_prep",
    )(rw, halo, *consts)


def _wkv_kernel(r_ref, k_ref, v_ref, kk_ref, bv_ref, lw_ref, o_ref, g_ref):
    C = CHUNK
    sgn = 1 - 2 * pl.program_id(1)

    @pl.when(pl.program_id(3) == 0)
    def _():
        g_ref[...] = jnp.zeros_like(g_ref)

    rc = lax.broadcasted_iota(jnp.int32, (C, C), 0)
    cc = lax.broadcasted_iota(jnp.int32, (C, C), 1)
    tri = jnp.where((rc - cc) * sgn >= 0, 1.0, 0.0).astype(BF16)
    lw = lw_ref[0]
    lw_hi, lw_lo = _split(lw)
    cum = _mm(tri, lw_hi) + _mm(tri, lw_lo)
    total = jnp.sum(lw, axis=0, keepdims=True)
    e_neg = jnp.exp(-cum)
    e_end = jnp.exp(total - cum)
    kk, bv, k, r = kk_ref[...], bv_ref[...], k_ref[...], r_ref[...]

    lane_head = lax.broadcasted_iota(jnp.int32, (C, GW), 1) // RWKV_HEAD

    def stack(x):
        return jnp.concatenate([jnp.where(lane_head == h, x, 0.0) for h in range(GROUP)], axis=0).astype(BF16)

    def fold(x):
        return x[0:C] + x[C:2 * C] + x[2 * C:3 * C] + x[3 * C:4 * C]

    ms_a = stack(-kk * jnp.exp(cum - lw))
    ms_r = stack(r * jnp.exp(cum))
    ms_b = stack(bv * e_neg)
    ms_k = stack(k * e_neg)
    ms_b2 = stack(bv * e_end)
    ms_k2 = stack(k * e_end)
    ms_v = stack(v_ref[...])

    n = GROUP * C
    ri = lax.broadcasted_iota(jnp.int32, (n, n), 0)
    ci = lax.broadcasted_iota(jnp.int32, (n, n), 1)
    same = (ri // C) == (ci // C)
    ahead = (ri - ci) * sgn
    strict = same & (ahead > 0)
    incl = same & (ahead >= 0)
    eye = ri == ci

    pm = _mm_nt(jnp.concatenate([ms_a, ms_r], axis=0), jnp.concatenate([ms_b, ms_k], axis=0))
    m_ab = jnp.where(strict, pm[0:n, 0:n], 0.0)
    m_ak = jnp.where(strict, pm[0:n, n:2 * n], 0.0).astype(BF16)
    m_rb = jnp.where(incl, pm[n:2 * n, 0:n], 0.0).astype(BF16)
    m_rk = jnp.where(incl, pm[n:2 * n, n:2 * n], 0.0).astype(BF16)

    npow = m_ab
    tinv = jnp.where(eye, 1.0, m_ab)
    for _ in range(int(math.log2(C)) - 1):
        nb = npow.astype(BF16)
        npow = _mm(nb, nb)
        tinv = tinv + _mm(tinv.astype(BF16), npow.astype(BF16))

    w1 = _mm(m_ak, ms_v)
    au = _mm(tinv.astype(BF16), jnp.concatenate([ms_a, w1.astype(BF16)], axis=1)).astype(BF16)
    ro = _mm(m_rb, au)
    r2 = fold(ms_r.astype(F32) + ro[:, 0:GW])
    o0 = fold(ro[:, GW:2 * GW] + _mm(m_rk, ms_v))
    phi = _mm_tn(ms_b2, au[:, 0:GW])
    psi = _mm_tn(jnp.concatenate([ms_b2, ms_k2], axis=0), jnp.concatenate([au[:, GW:2 * GW], ms_v], axis=0))

    g = g_ref[...]
    res = _mm(jnp.concatenate([r2, phi], axis=0).astype(BF16), g.astype(BF16))
    o_ref[0] = res[0:C] + o0
    p_end = jnp.exp(total)
    gi = lax.broadcasted_iota(jnp.int32, (GW, GW), 0)
    gj = lax.broadcasted_iota(jnp.int32, (GW, GW), 1)
    p_col = jnp.sum(jnp.where(gi == gj, p_end, 0.0), axis=1, keepdims=True)
    g_ref[...] = p_col * g + res[C:C + GW] + psi


def _wkv(r, k, v, kk, bv, lw, B, S):
    T = r.shape[0]
    C = CHUNK
    nc = S // C
    ng = RWKV_WIDTH // GW

    def tok(b, d, g, c):
        return b * nc + jnp.where(d == 0, c, nc - 1 - c)

    row = pl.BlockSpec((C, GW), lambda b, d, g, c: (tok(b, d, g, c), g))
    row_d = pl.BlockSpec((1, C, GW), lambda b, d, g, c: (d, tok(b, d, g, c), g))
    return pl.pallas_call(
        _wkv_kernel,
        grid=(B, 2, ng, nc),
        in_specs=[row, row, row, row, row, row_d],
        out_specs=row_d,
        out_shape=jax.ShapeDtypeStruct((2, T, RWKV_WIDTH), F32),
        scratch_shapes=[pltpu.VMEM((GW, GW), F32)],
        compiler_params=_params(("parallel", "parallel", "parallel", "arbitrary")),
        name="wkv",
    )(r, k, v, kk, bv, lw)


def _mix_kernel(o_ref, bonus_ref, g_ref, oat_ref, gate_ref, x_ref, bo_ref, lng_ref, lnb_ref, woa_ref,
                wob_ref, wm_ref, out_ref):
    bo = bo_ref[...]
    bonus = bonus_ref[...]
    ob = None
    for d in range(2):
        o = o_ref[d]
        mu = _mm_hl(o, bo) / RWKV_HEAD
        dl = o - mu
        var = _mm_hl(dl * dl, bo) / RWKV_HEAD
        y = dl * lax.rsqrt(var + GN_EPS) * lng_ref[...] + lnb_ref[...] + bonus
        y = y * g_ref[d]
        ob = y if ob is None else ob + y
    y_b = _mm(ob.astype(BF16), wob_ref[...])
    y_a = _mm_tn(oat_ref[0], woa_ref[...])
    mixed = gate_ref[:, 0:D_MODEL] * y_a + gate_ref[:, D_MODEL:2 * D_MODEL] * y_b
    out_ref[...] = x_ref[...] + _mm(mixed.astype(BF16), wm_ref[...])


def _mix(o, bonus, g, oat, gates, x2, bo, lng, lnb, woa, wob, wm, B, S, tm):
    T = x2.shape[0]
    W = RWKV_WIDTH
    nj = S // tm
    consts = [bo, lng, lnb, woa, wob, wm]
    return pl.pallas_call(
        _mix_kernel,
        grid=(B, nj),
        in_specs=[pl.BlockSpec((2, tm, W), lambda b, j: (0, b * nj + j, 0)),
                  pl.BlockSpec((tm, W), lambda b, j: (b * nj + j, 0)),
                  pl.BlockSpec((2, tm, W), lambda b, j: (0, b * nj + j, 0)),
                  pl.BlockSpec((1, MLA_HEADS * V_HEAD, tm), lambda b, j: (b, 0, j)),
                  pl.BlockSpec((tm, 2 * D_MODEL), lambda b, j: (b * nj + j, 0)),
                  pl.BlockSpec((tm, D_MODEL), lambda b, j: (b * nj + j, 0))]
        + [_const_spec(c.shape) for c in consts],
        out_specs=pl.BlockSpec((tm, D_MODEL), lambda b, j: (b * nj + j, 0)),
        out_shape=jax.ShapeDtypeStruct((T, D_MODEL), F32),
        compiler_params=_params(("parallel", "parallel")),
        name="mix",
    )(o, bonus, g, oat, gates, x2, *consts)


def _ffn_kernel(x_ref, halo_ref, g_ref, wg_ref, wu_ref, cw_ref, cb_ref, wd_ref, out_ref, *, th):
    tm = x_ref.shape[0]
    x = x_ref[...]
    h = _rms(x, g_ref[...]).astype(BF16)
    hh = _rms(halo_ref[0], g_ref[...]).astype(BF16)
    acc = x
    for j in range(FFN_HIDDEN // th):
        sl = slice(j * th, (j + 1) * th)
        pre = _mm(h, wg_ref[:, sl])
        prev, nxt = _shifted(pre, _mm(hh, wg_ref[:, sl]), tm)
        gp = prev * cw_ref[0:1, sl] + pre * cw_ref[1:2, sl] + nxt * cw_ref[2:3, sl] + cb_ref[:, sl]
        act = gp * _sigmoid(gp) * _mm(h, wu_ref[:, sl])
        acc = acc + _mm(act.astype(BF16), wd_ref[sl, :])
    out_ref[...] = acc


def _ffn(x1, halo, g, wg, wu, cw, cb, wd, tm, th):
    T = x1.shape[0]
    consts = [g, wg, wu, cw, cb, wd]
    return pl.pallas_call(
        functools.partial(_ffn_kernel, th=th),
        grid=(T // tm,),
        in_specs=[pl.BlockSpec((tm, D_MODEL), lambda i: (i, 0)), pl.BlockSpec((1, 8, D_MODEL), lambda i: (i, 0, 0))]
        + [_const_spec(c.shape) for c in consts],
        out_specs=pl.BlockSpec((tm, D_MODEL), lambda i: (i, 0)),
        out_shape=jax.ShapeDtypeStruct((T, D_MODEL), F32),
        compiler_params=_params(("parallel",)),
        name="ffn",
    )(x1, halo, *consts)


def _halo_rows(a, tm, S):
    T, W = a.shape
    nt = T // tm
    zero = jnp.zeros((1, W), a.dtype)
    prev = jnp.concatenate([zero, a[tm - 1::tm][:-1]], axis=0)
    nxt = jnp.concatenate([a[tm::tm], zero], axis=0)
    start = (jnp.arange(nt) * tm) % S
    prev = jnp.where((start == 0)[:, None], 0.0, prev)
    nxt = jnp.where((start == S - tm)[:, None], 0.0, nxt)
    return jnp.concatenate([prev[:, None], nxt[:, None], jnp.zeros((nt, 6, W), a.dtype)], axis=1)


def _pad_cols(w, width):
    return jnp.pad(w, ((0, 0), (0, width - w.shape[1])))


def _pad_rows(w, height):
    return jnp.pad(w, ((0, height - w.shape[0]), (0, 0)))


def _rw_layout(t):
    W = RWKV_WIDTH
    pad = lambda a: jnp.pad(a, [(0, 0)] * (a.ndim - 1) + [(0, LANE - a.shape[-1])])
    o = 3 * W
    return jnp.concatenate([t[..., 0:o], pad(t[..., o:o + 64]), pad(t[..., o + 64:o + 128]),
                            pad(t[..., o + 128:o + 192]), t[..., o + 192:o + 448]], axis=-1)


def kernel(x, positions, norm_mix_g, w_in, b_gate, q_a_norm_g, kv_a_norm_g, w_uq, w_ukv, qn_norm_g, qr_norm_g,
           kn_norm_g, kr_norm_g, shift_mu, w0, w2, a0, a2, g2, k_k, k_a, r_k, ln_x_g, ln_x_b, w_o, w_merge,
           norm_ffn_g, w_ffn_gate, w_ffn_up, ffn_conv_w, ffn_conv_b, w_ffn_down):
    B, S, D = x.shape
    T = B * S
    depth = norm_mix_g.shape[0]
    H = MLA_HEADS
    mla_cols = Q_LORA + KV_LORA + QK_ROPE
    rw_cols = 3 * RWKV_WIDTH + A_LORA + 2 * DECAY_LORA + 2 * GATE_LORA

    inv_freq = ROPE_THETA ** (-jnp.arange(0, QK_ROPE, 2, dtype=F32) / QK_ROPE)
    freq = jnp.zeros((1, LANE), F32).at[0, QK_NOPE:QK_NOPE + QK_ROPE].set(jnp.concatenate([inv_freq, inv_freq]))
    head_of = jnp.arange(RWKV_WIDTH) // RWKV_HEAD
    bo = (head_of[:, None] == head_of[None, :]).astype(BF16)
    pos = positions.reshape(T, 1)
    row = lambda v: v.reshape(1, -1).astype(F32)

    x2 = x.reshape(T, D)
    for l in range(depth):
        wi = w_in[l]
        w_mla = jnp.concatenate([wi[:, 0:Q_LORA + KV_LORA], jnp.zeros((D, QK_NOPE), F32),
                                 wi[:, Q_LORA + KV_LORA:mla_cols], jnp.zeros((D, LANE - QK_NOPE - QK_ROPE), F32)], axis=1)
        w_rw = _rw_layout(wi[:, mla_cols:mla_cols + rw_cols])
        w_gate = wi[:, mla_cols + rw_cols:]
        mla, rw, gates = _inproj(x2, row(norm_mix_g[l]), w_mla.astype(BF16), w_rw.astype(BF16), w_gate.astype(BF16),
                                 b_gate[l].reshape(1, 2 * D), tm=256)

        wq = _pad_cols(w_uq[l].reshape(Q_LORA, H, QK_NOPE + QK_ROPE).transpose(1, 0, 2).reshape(H * Q_LORA, -1), LANE)
        wq = wq.reshape(H, Q_LORA, LANE).astype(BF16)
        wkv = w_ukv[l].reshape(KV_LORA, H, QK_NOPE + V_HEAD)
        wk = _pad_cols(wkv[:, :, :QK_NOPE].transpose(1, 0, 2).reshape(H * KV_LORA, QK_NOPE), LANE)
        wk = wk.reshape(H, KV_LORA, LANE).astype(BF16)
        wvt = wkv[:, :, QK_NOPE:].reshape(KV_LORA, H * V_HEAD).T.astype(BF16)
        gqh = _pad_cols(jnp.concatenate([qn_norm_g[l], qr_norm_g[l]]).reshape(1, -1), LANE)
        gkn = _pad_cols(kn_norm_g[l].reshape(1, -1), LANE)
        gkr = _pad_cols(jnp.concatenate([jnp.zeros((QK_NOPE,), F32), kr_norm_g[l]]).reshape(1, -1), LANE)
        q, k, vt = _mla_prep(mla, pos, row(q_a_norm_g[l]), row(kv_a_norm_g[l]), wq, wk, wvt, gqh, gkn, gkr, freq,
                             B, S, tm=512)
        oat = _attn(q, k, vt, tq=256)

        halo = _halo_rows(rw, 256, S)
        a2p = _pad_rows(a2[l], LANE).astype(BF16)
        w2p = jnp.pad(w2[l], ((0, 0), (0, LANE - DECAY_LORA), (0, 0))).astype(BF16)
        r_, k_, v_, kk_, bv_, lw_, bonus, g_ = _rwkv_prep(
            rw, halo, _rw_layout(shift_mu[l]), row(a0[l]), a2p, w0[l], w2p, g2[l].astype(BF16), row(k_k[l]),
            row(k_a[l]), row(r_k[l]), bo, tm=256)
        o_scan = _wkv(r_, k_, v_, kk_, bv_, lw_, B, S)

        wo = w_o[l].astype(BF16)
        x2 = _mix(o_scan, bonus, g_, oat, gates, x2, bo, row(ln_x_g[l]), row(ln_x_b[l]), wo[:H * V_HEAD],
                  wo[H * V_HEAD:], w_merge[l].astype(BF16), B, S, tm=256)

        x2 = _ffn(x2, _halo_rows(x2, 512, S), row(norm_ffn_g[l]), w_ffn_gate[l].astype(BF16),
                  w_ffn_up[l].astype(BF16), ffn_conv_w[l], ffn_conv_b[l].reshape(1, -1),
                  w_ffn_down[l].astype(BF16), tm=512, th=1408)
    return x2.reshape(B, S, D)
```

```python
import functools
import math

import jax
import jax.numpy as jnp
from jax import lax
from jax.experimental import pallas as pl
from jax.experimental.pallas import tpu as pltpu

F32 = jnp.float32
BF16 = jnp.bfloat16

D_MODEL = 1024
MLA_HEADS = 8
QK_NOPE = 64
QK_ROPE = 32
V_HEAD = 64
Q_LORA = 256
KV_LORA = 128
ROPE_THETA = 10000.0
RWKV_HEADS = 8
RWKV_HEAD = 64
RWKV_WIDTH = RWKV_HEADS * RWKV_HEAD
DECAY_LORA = 64
A_LORA = 64
GATE_LORA = 128
GN_EPS = 64e-5
FFN_HIDDEN = 2816
NORM_EPS = 1e-6

LANE = 128
MLA_W = 512
RW_W = 3 * RWKV_WIDTH + 5 * LANE
VMEM_LIMIT = 56 * 1024 * 1024

CHUNK = 64
GROUP = 4
GW = GROUP * RWKV_HEAD


def _mm(a, b):
    return jnp.dot(a, b, preferred_element_type=F32)


def _mm_nt(a, b):
    return lax.dot_general(a, b, (((1,), (1,)), ((), ())), preferred_element_type=F32)


def _mm_tn(a, b):
    return lax.dot_general(a, b, (((0,), (0,)), ((), ())), preferred_element_type=F32)


def _split(x):
    hi = x.astype(BF16)
    lo = (x - hi.astype(F32)).astype(BF16)
    return hi, lo


def _mm_hl(x, b01):
    hi, lo = _split(x)
    return _mm(hi, b01) + _mm(lo, b01)


def _sigmoid(x):
    return 1.0 / (1.0 + jnp.exp(-x))


def _rms(x, g, eps=NORM_EPS):
    return x * lax.rsqrt(jnp.mean(x * x, axis=-1, keepdims=True) + eps) * g


def _const_spec(shape):
    nd = len(shape)
    return pl.BlockSpec(shape, lambda *_: (0,) * nd, pipeline_mode=pl.Buffered(1))


def _params(sem):
    return pltpu.CompilerParams(dimension_semantics=sem, vmem_limit_bytes=VMEM_LIMIT)


def _inproj_kernel(x_ref, g_ref, wm_ref, wr_ref, wg_ref, bg_ref, mla_ref, rw_ref, gate_ref):
    h = _rms(x_ref[...], g_ref[...]).astype(BF16)
    mla_ref[...] = _mm(h, wm_ref[...])
    rw_ref[...] = _mm(h, wr_ref[...])
    gate_ref[...] = _sigmoid(_mm(h, wg_ref[...]) + bg_ref[...]).astype(gate_ref.dtype)


def _inproj(x2, g, wm, wr, wg, bg, tm):
    T = x2.shape[0]
    row = lambda w: pl.BlockSpec((tm, w), lambda i: (i, 0))
    return pl.pallas_call(
        _inproj_kernel,
        grid=(T // tm,),
        in_specs=[row(D_MODEL), _const_spec(g.shape), _const_spec(wm.shape), _const_spec(wr.shape),
                  _const_spec(wg.shape), _const_spec(bg.shape)],
        out_specs=[row(MLA_W), row(RW_W), row(2 * D_MODEL)],
        out_shape=[jax.ShapeDtypeStruct((T, MLA_W), F32), jax.ShapeDtypeStruct((T, RW_W), F32),
                   jax.ShapeDtypeStruct((T, 2 * D_MODEL), BF16)],
        compiler_params=_params(("parallel",)),
        name="inproj",
    )(x2, g, wm, wr, wg, bg)


def _rotary(t, cos, sin, lane):
    first = (lane >= QK_NOPE) & (lane < QK_NOPE + QK_ROPE // 2)
    second = (lane >= QK_NOPE + QK_ROPE // 2) & (lane < QK_NOPE + QK_ROPE)
    up = pltpu.roll(t, LANE - QK_ROPE // 2, axis=1)
    dn = pltpu.roll(t, QK_ROPE // 2, axis=1)
    rot = jnp.where(first, -up, jnp.where(second, dn, 0.0))
    return t * cos + rot * sin


def _mla_prep_kernel(mla_ref, pos_ref, gq_ref, gkv_ref, wq_ref, wk_ref, wvt_ref, gqh_ref, gkn_ref,
                     gkr_ref, freq_ref, q_ref, k_ref, vt_ref):
    tm = mla_ref.shape[0]
    lane = lax.broadcasted_iota(jnp.int32, (tm, LANE), 1)
    nope = lane < QK_NOPE
    ang = pos_ref[...].astype(F32) * freq_ref[...]
    cos, sin = jnp.cos(ang), jnp.sin(ang)
    cq = _rms(mla_ref[:, 0:Q_LORA], gq_ref[...]).astype(BF16)
    ckv = _rms(mla_ref[:, Q_LORA:Q_LORA + KV_LORA], gkv_ref[...]).astype(BF16)
    kr = mla_ref[:, Q_LORA + KV_LORA:MLA_W]
    kr = kr * lax.rsqrt(jnp.sum(kr * kr, axis=-1, keepdims=True) / QK_ROPE + NORM_EPS) * gkr_ref[...]
    kr = _rotary(kr, cos, sin, lane)
    scale = math.log2(math.e) / math.sqrt(QK_NOPE + QK_ROPE)
    for h in range(MLA_HEADS):
        q = _mm(cq, wq_ref[h])
        q2 = q * q
        ss_n = jnp.sum(jnp.where(nope, q2, 0.0), axis=-1, keepdims=True)
        ss_r = jnp.sum(jnp.where(nope, 0.0, q2), axis=-1, keepdims=True)
        inv = jnp.where(nope, lax.rsqrt(ss_n / QK_NOPE + NORM_EPS), lax.rsqrt(ss_r / QK_ROPE + NORM_EPS))
        q = _rotary(q * inv * gqh_ref[...], cos, sin, lane)
        q_ref[0, h] = (q * scale).astype(BF16)
        kn = _mm(ckv, wk_ref[h])
        kn = kn * lax.rsqrt(jnp.sum(kn * kn, axis=-1, keepdims=True) / QK_NOPE + NORM_EPS) * gkn_ref[...]
        k_ref[0, h] = (kn + kr).astype(BF16)
    vt_ref[0] = _mm_nt(wvt_ref[...], ckv).astype(BF16)


def _mla_prep(mla, pos, gq, gkv, wq, wk, wvt, gqh, gkn, gkr, freq, B, S, tm):
    nj = S // tm
    H = MLA_HEADS
    consts = [gq, gkv, wq, wk, wvt, gqh, gkn, gkr, freq]
    return pl.pallas_call(
        _mla_prep_kernel,
        grid=(B, nj),
        in_specs=[pl.BlockSpec((tm, MLA_W), lambda b, j: (b * nj + j, 0)),
                  pl.BlockSpec((tm, 1), lambda b, j: (b * nj + j, 0))] + [_const_spec(c.shape) for c in consts],
        out_specs=[pl.BlockSpec((1, H, tm, LANE), lambda b, j: (b, 0, j, 0)),
                   pl.BlockSpec((1, H, tm, LANE), lambda b, j: (b, 0, j, 0)),
                   pl.BlockSpec((1, H * V_HEAD, tm), lambda b, j: (b, 0, j))],
        out_shape=[jax.ShapeDtypeStruct((B, H, S, LANE), BF16), jax.ShapeDtypeStruct((B, H, S, LANE), BF16),
                   jax.ShapeDtypeStruct((B, H * V_HEAD, S), BF16)],
        compiler_params=_params(("parallel", "parallel")),
        name="mla_prep",
    )(mla, pos, *consts)


def _attn_kernel(q_ref, k_ref, vt_ref, ot_ref, *, tk):
    q = q_ref[0, 0]
    tq = q.shape[0]
    S = k_ref.shape[2]

    m = jnp.full((1, tq), -jnp.inf, F32)
    l = jnp.zeros((1, tq), F32)
    acc = jnp.zeros((V_HEAD, tq), F32)
    nblk = S // tk
    ahead = 10
    scores = [_mm_nt(k_ref[0, 0, j * tk:(j + 1) * tk, :], q) for j in range(min(ahead, nblk))]
    for j in range(nblk):
        if j + ahead < nblk:
            scores.append(_mm_nt(k_ref[0, 0, (j + ahead) * tk:(j + ahead + 1) * tk, :], q))
        s = scores[j]
        m_new = jnp.maximum(m, jnp.max(s, axis=0, keepdims=True))
        p = jnp.exp2(s - m_new)
        alpha = jnp.exp2(m - m_new)
        l = alpha * l + jnp.sum(p, axis=0, keepdims=True)
        acc = alpha * acc + _mm(vt_ref[0, :, j * tk:(j + 1) * tk], p.astype(BF16))
        m = m_new
    ot_ref[0] = (acc / l).astype(BF16)


def _attn(q, k, vt, tq, tk):
    B, H, S, _ = q.shape
    return pl.pallas_call(
        functools.partial(_attn_kernel, tk=tk),
        grid=(B, H, S // tq),
        in_specs=[pl.BlockSpec((1, 1, tq, LANE), lambda b, h, i: (b, h, i, 0)),
                  pl.BlockSpec((1, 1, S, LANE), lambda b, h, i: (b, h, 0, 0)),
                  pl.BlockSpec((1, V_HEAD, S), lambda b, h, i: (b, h, 0))],
        out_specs=pl.BlockSpec((1, V_HEAD, tq), lambda b, h, i: (b, h, i)),
        out_shape=jax.ShapeDtypeStruct((B, H * V_HEAD, S), BF16),
        compiler_params=_params(("parallel", "parallel", "parallel")),
        name="attn",
    )(q, k, vt)


SUBLANE = 8


def _shifted(p, before, after):
    tm = p.shape[0]
    row = lax.broadcasted_iota(jnp.int32, p.shape, 0)
    prev = jnp.where(row == 0, before, pltpu.roll(p, 1, axis=0))
    nxt = jnp.where(row == tm - 1, after, pltpu.roll(p, tm - 1, axis=0))
    return prev, nxt


def _halo_specs(tm, width, T):
    per = tm // SUBLANE
    before = pl.BlockSpec((SUBLANE, width), lambda i: (jnp.maximum(i * per - 1, 0), 0))
    after = pl.BlockSpec((SUBLANE, width), lambda i: (jnp.minimum((i + 1) * per, T // SUBLANE - 1), 0))
    return [before, after]


def _halo_rows(before_ref, after_ref, tiles_per_seq):
    j = pl.program_id(0) % tiles_per_seq
    before = jnp.where(j == 0, 0.0, before_ref[SUBLANE - 1:SUBLANE, :])
    after = jnp.where(j == tiles_per_seq - 1, 0.0, after_ref[0:1, :])
    return before, after


def _rwkv_prep_kernel(rw_ref, before_ref, after_ref, mu_ref, a0_ref, a2_ref, w0_ref, w2_ref, g2_ref, kk_ref, ka_ref,
                      rk_ref, bo_ref, r_out, k_out, v_out, kk_out, bv_out, bonus_out, lwf_out, lwb_out, gf_out,
                      gb_out, *, tiles_per_seq):
    p = rw_ref[...]
    prev, nxt = _shifted(p, *_halo_rows(before_ref, after_ref, tiles_per_seq))
    u = p + mu_ref[0:1, :] * (prev - p) + mu_ref[1:2, :] * (nxt - p)
    W = RWKV_WIDTH
    r, k, v = u[:, 0:W], u[:, W:2 * W], u[:, 2 * W:3 * W]
    a_lo = u[:, 3 * W:3 * W + LANE]
    a = _sigmoid(a0_ref[...] + _mm(a_lo.astype(BF16), a2_ref[...]))
    bo = bo_ref[...]
    kkf = k * kk_ref[...]
    kkn = kkf / jnp.maximum(jnp.sqrt(_mm_hl(kkf * kkf, bo)), 1e-12)
    k = k * (1.0 + (a - 1.0) * ka_ref[...])
    r_out[...] = r
    k_out[...] = k
    v_out[...] = v
    kk_out[...] = kkn
    bv_out[...] = kkn * a
    bonus_out[...] = _mm_hl(r * k * rk_ref[...], bo) * v
    for d, (lw_out, g_out) in enumerate(((lwf_out, gf_out), (lwb_out, gb_out))):
        dlo = u[:, 3 * W + (1 + d) * LANE:3 * W + (2 + d) * LANE]
        glo = u[:, 3 * W + (3 + d) * LANE:3 * W + (4 + d) * LANE]
        z = -(w0_ref[d:d + 1, :] + _mm(jnp.tanh(dlo).astype(BF16), w2_ref[d]))
        softplus = jnp.maximum(z, 0.0) + jnp.log(1.0 + jnp.exp(-jnp.abs(z)))
        lw_out[...] = -jnp.exp(-softplus - 0.5)
        g_out[...] = _mm(_sigmoid(glo).astype(BF16), g2_ref[d])


def _rwkv_prep(rw, mu, a0, a2, w0, w2, g2, k_k, k_a, r_k, bo, S, tm):
    T = rw.shape[0]
    W = RWKV_WIDTH
    consts = [mu, a0, a2, w0, w2, g2, k_k, k_a, r_k, bo]
    row = pl.BlockSpec((tm, W), lambda i: (i, 0))
    one = jax.ShapeDtypeStruct((T, W), F32)
    return pl.pallas_call(
        functools.partial(_rwkv_prep_kernel, tiles_per_seq=S // tm),
        grid=(T // tm,),
        in_specs=[pl.BlockSpec((tm, RW_W), lambda i: (i, 0))] + _halo_specs(tm, RW_W, T)
        + [_const_spec(c.shape) for c in consts],
        out_specs=[row] * 10,
        out_shape=[one] * 10,
        compiler_params=_params(("parallel",)),
        name="rwkv_prep",
    )(rw, rw, rw, *consts)


def _wkv_chain(r, k, v, kk, bv, lw, g, reverse):
    C = CHUNK
    sgn = -1 if reverse else 1
    rc = lax.broadcasted_iota(jnp.int32, (C, C), 0)
    cc = lax.broadcasted_iota(jnp.int32, (C, C), 1)
    tri = jnp.where((rc - cc) * sgn >= 0, 1.0, 0.0).astype(BF16)
    lw_hi, lw_lo = _split(lw)
    cum = _mm(tri, lw_hi) + _mm(tri, lw_lo)
    total = jnp.sum(lw, axis=0, keepdims=True)
    e_neg = jnp.exp(-cum)
    e_end = jnp.exp(total - cum)

    lane_head = lax.broadcasted_iota(jnp.int32, (C, GW), 1) // RWKV_HEAD

    def stack(x):
        return jnp.concatenate([jnp.where(lane_head == h, x, 0.0) for h in range(GROUP)], axis=0).astype(BF16)

    def fold(x):
        return x[0:C] + x[C:2 * C] + x[2 * C:3 * C] + x[3 * C:4 * C]

    ms_a = stack(-kk * jnp.exp(cum - lw))
    ms_r = stack(r * jnp.exp(cum))
    ms_b = stack(bv * e_neg)
    ms_k = stack(k * e_neg)
    ms_b2 = stack(bv * e_end)
    ms_k2 = stack(k * e_end)
    ms_v = stack(v)

    n = GROUP * C
    ri = lax.broadcasted_iota(jnp.int32, (n, n), 0)
    ci = lax.broadcasted_iota(jnp.int32, (n, n), 1)
    same = (ri // C) == (ci // C)
    ahead = (ri - ci) * sgn
    strict = same & (ahead > 0)
    incl = same & (ahead >= 0)
    eye = ri == ci

    yield
    pm = _mm_nt(jnp.concatenate([ms_a, ms_r], axis=0), jnp.concatenate([ms_b, ms_k], axis=0))
    m_ab = jnp.where(strict, pm[0:n, 0:n], 0.0)
    m_ak = jnp.where(strict, pm[0:n, n:2 * n], 0.0).astype(BF16)
    m_rb = jnp.where(incl, pm[n:2 * n, 0:n], 0.0).astype(BF16)
    m_rk = jnp.where(incl, pm[n:2 * n, n:2 * n], 0.0).astype(BF16)
    yield
    w1 = _mm(m_ak, ms_v)
    o0 = _mm(m_rk, ms_v)

    npow = m_ab
    tinv = jnp.where(eye, 1.0, m_ab)
    for _ in range(int(math.log2(C)) - 1):
        yield
        nb = npow.astype(BF16)
        npow = _mm(nb, nb)
        yield
        tinv = tinv + _mm(tinv.astype(BF16), npow.astype(BF16))

    yield
    au = _mm(tinv.astype(BF16), jnp.concatenate([ms_a, w1.astype(BF16)], axis=1)).astype(BF16)
    yield
    ro = _mm(m_rb, au)
    r2 = fold(ms_r.astype(F32) + ro[:, 0:GW])
    o0 = fold(ro[:, GW:2 * GW] + o0)
    yield
    phi = _mm_tn(ms_b2, au[:, 0:GW])
    yield
    psi = _mm_tn(jnp.concatenate([ms_b2, ms_k2], axis=0), jnp.concatenate([au[:, GW:2 * GW], ms_v], axis=0))
    yield
    res = _mm(jnp.concatenate([r2, phi], axis=0).astype(BF16), g.astype(BF16))
    gi = lax.broadcasted_iota(jnp.int32, (GW, GW), 0)
    gj = lax.broadcasted_iota(jnp.int32, (GW, GW), 1)
    p_col = jnp.sum(jnp.where(gi == gj, jnp.exp(total), 0.0), axis=1, keepdims=True)
    yield res[0:C] + o0, p_col * g + res[C:C + GW] + psi


def _wkv_kernel(rf, kf, vf, kkf, bvf, lwf, rb, kb, vb, kkb, bvb, lwb, of_ref, ob_ref, g_ref):
    @pl.when(pl.program_id(1) == 0)
    def _():
        g_ref[...] = jnp.zeros_like(g_ref)

    dirs = ((rf, kf, vf, kkf, bvf, lwf, of_ref), (rb, kb, vb, kkb, bvb, lwb, ob_ref))
    chains = []
    for d, (*ins, o_ref) in enumerate(dirs):
        for gidx in range(RWKV_WIDTH // GW):
            sl = slice(gidx * GW, (gidx + 1) * GW)
            gen = _wkv_chain(*(ref[:, sl] for ref in ins), g_ref[d, gidx], reverse=d == 1)
            chains.append((gen, o_ref, sl, d, gidx))
    while chains:
        for chain in list(chains):
            gen, o_ref, sl, d, gidx = chain
            out = next(gen)
            if out is not None:
                o_ref[:, sl], g_ref[d, gidx] = out
                chains.remove(chain)


def _wkv(r, k, v, kk, bv, lwf, lwb, B, S):
    T, W = r.shape
    C = CHUNK
    nc = S // C
    fwd = pl.BlockSpec((C, W), lambda b, c: (b * nc + c, 0))
    bwd = pl.BlockSpec((C, W), lambda b, c: (b * nc + nc - 1 - c, 0))
    out = jax.ShapeDtypeStruct((T, W), F32)
    return pl.pallas_call(
        _wkv_kernel,
        grid=(B, nc),
        in_specs=[fwd] * 6 + [bwd] * 6,
        out_specs=[fwd, bwd],
        out_shape=[out, out],
        scratch_shapes=[pltpu.VMEM((2, W // GW, GW, GW), F32)],
        compiler_params=_params(("parallel", "arbitrary")),
        name="wkv",
    )(r, k, v, kk, bv, lwf, r, k, v, kk, bv, lwb)


def _mix_kernel(of_ref, ob_ref, bonus_ref, gf_ref, gb_ref, oat_ref, gate_ref, x_ref, bo_ref, lng_ref, lnb_ref,
                woa_ref, wob_ref, wm_ref, out_ref):
    bo = bo_ref[...]
    bonus = bonus_ref[...]
    ob = None
    for o_ref, g_ref in ((of_ref, gf_ref), (ob_ref, gb_ref)):
        o = o_ref[...]
        mu = _mm_hl(o, bo) / RWKV_HEAD
        dl = o - mu
        var = _mm_hl(dl * dl, bo) / RWKV_HEAD
        y = dl * lax.rsqrt(var + GN_EPS) * lng_ref[...] + lnb_ref[...] + bonus
        y = y * g_ref[...]
        ob = y if ob is None else ob + y
    y_b = _mm(ob.astype(BF16), wob_ref[...])
    y_a = _mm_tn(oat_ref[0], woa_ref[...])
    mixed = gate_ref[:, 0:D_MODEL] * y_a + gate_ref[:, D_MODEL:2 * D_MODEL] * y_b
    out_ref[...] = x_ref[...] + _mm(mixed.astype(BF16), wm_ref[...])


def _mix(o_f, o_b, bonus, g_f, g_b, oat, gates, x2, bo, lng, lnb, woa, wob, wm, B, S, tm):
    T = x2.shape[0]
    W = RWKV_WIDTH
    nj = S // tm
    consts = [bo, lng, lnb, woa, wob, wm]
    tok = pl.BlockSpec((tm, W), lambda b, j: (b * nj + j, 0))
    return pl.pallas_call(
        _mix_kernel,
        grid=(B, nj),
        in_specs=[tok, tok, tok, tok, tok,
                  pl.BlockSpec((1, MLA_HEADS * V_HEAD, tm), lambda b, j: (b, 0, j)),
                  pl.BlockSpec((tm, 2 * D_MODEL), lambda b, j: (b * nj + j, 0)),
                  pl.BlockSpec((tm, D_MODEL), lambda b, j: (b * nj + j, 0))]
        + [_const_spec(c.shape) for c in consts],
        out_specs=pl.BlockSpec((tm, D_MODEL), lambda b, j: (b * nj + j, 0)),
        out_shape=jax.ShapeDtypeStruct((T, D_MODEL), F32),
        compiler_params=_params(("parallel", "parallel")),
        name="mix",
    )(o_f, o_b, bonus, g_f, g_b, oat, gates, x2, *consts)


def _ffn_kernel(x_ref, before_ref, after_ref, g_ref, wg_ref, wu_ref, cw_ref, cb_ref, wd_ref, out_ref, *, th,
                tiles_per_seq):
    x = x_ref[...]
    h = _rms(x, g_ref[...]).astype(BF16)
    before, after = _halo_rows(before_ref, after_ref, tiles_per_seq)
    edge = jnp.concatenate([jnp.broadcast_to(before, (SUBLANE, D_MODEL)), jnp.broadcast_to(after, (SUBLANE, D_MODEL))], axis=0)
    hh = _rms(edge, g_ref[...]).astype(BF16)
    acc = x
    for j in range(FFN_HIDDEN // th):
        sl = slice(j * th, (j + 1) * th)
        pre = _mm(h, wg_ref[:, sl])
        pre_edge = _mm(hh, wg_ref[:, sl])
        prev, nxt = _shifted(pre, pre_edge[0:1, :], pre_edge[SUBLANE:SUBLANE + 1, :])
        gp = prev * cw_ref[0:1, sl] + pre * cw_ref[1:2, sl] + nxt * cw_ref[2:3, sl] + cb_ref[:, sl]
        act = gp * _sigmoid(gp) * _mm(h, wu_ref[:, sl])
        acc = acc + _mm(act.astype(BF16), wd_ref[sl, :])
    out_ref[...] = acc


def _ffn(x1, g, wg, wu, cw, cb, wd, S, tm, th):
    T = x1.shape[0]
    consts = [g, wg, wu, cw, cb, wd]
    return pl.pallas_call(
        functools.partial(_ffn_kernel, th=th, tiles_per_seq=S // tm),
        grid=(T // tm,),
        in_specs=[pl.BlockSpec((tm, D_MODEL), lambda i: (i, 0))] + _halo_specs(tm, D_MODEL, T)
        + [_const_spec(c.shape) for c in consts],
        out_specs=pl.BlockSpec((tm, D_MODEL), lambda i: (i, 0)),
        out_shape=jax.ShapeDtypeStruct((T, D_MODEL), F32),
        compiler_params=_params(("parallel",)),
        name="ffn",
    )(x1, x1, x1, *consts)


def _pad_cols(w, width):
    return jnp.pad(w, ((0, 0), (0, width - w.shape[1])))


def _pad_rows(w, height):
    return jnp.pad(w, ((0, height - w.shape[0]), (0, 0)))


def _rw_layout(t):
    W = RWKV_WIDTH
    pad = lambda a: jnp.pad(a, [(0, 0)] * (a.ndim - 1) + [(0, LANE - a.shape[-1])])
    o = 3 * W
    return jnp.concatenate([t[..., 0:o], pad(t[..., o:o + 64]), pad(t[..., o + 64:o + 128]),
                            pad(t[..., o + 128:o + 192]), t[..., o + 192:o + 448]], axis=-1)


def kernel(x, positions, norm_mix_g, w_in, b_gate, q_a_norm_g, kv_a_norm_g, w_uq, w_ukv, qn_norm_g, qr_norm_g,
           kn_norm_g, kr_norm_g, shift_mu, w0, w2, a0, a2, g2, k_k, k_a, r_k, ln_x_g, ln_x_b, w_o, w_merge,
           norm_ffn_g, w_ffn_gate, w_ffn_up, ffn_conv_w, ffn_conv_b, w_ffn_down):
    B, S, D = x.shape
    T = B * S
    depth = norm_mix_g.shape[0]
    H = MLA_HEADS
    mla_cols = Q_LORA + KV_LORA + QK_ROPE
    rw_cols = 3 * RWKV_WIDTH + A_LORA + 2 * DECAY_LORA + 2 * GATE_LORA

    inv_freq = ROPE_THETA ** (-jnp.arange(0, QK_ROPE, 2, dtype=F32) / QK_ROPE)
    freq = jnp.zeros((1, LANE), F32).at[0, QK_NOPE:QK_NOPE + QK_ROPE].set(jnp.concatenate([inv_freq, inv_freq]))
    head_of = jnp.arange(RWKV_WIDTH) // RWKV_HEAD
    bo = (head_of[:, None] == head_of[None, :]).astype(BF16)
    pos = positions.reshape(T, 1)
    row = lambda v: v.reshape(1, -1).astype(F32)

    x2 = x.reshape(T, D)
    for l in range(depth):
        wi = w_in[l]
        w_mla = jnp.concatenate([wi[:, 0:Q_LORA + KV_LORA], jnp.zeros((D, QK_NOPE), F32),
                                 wi[:, Q_LORA + KV_LORA:mla_cols], jnp.zeros((D, LANE - QK_NOPE - QK_ROPE), F32)], axis=1)
        w_rw = _rw_layout(wi[:, mla_cols:mla_cols + rw_cols])
        w_gate = wi[:, mla_cols + rw_cols:]
        mla, rw, gates = _inproj(x2, row(norm_mix_g[l]), w_mla.astype(BF16), w_rw.astype(BF16), w_gate.astype(BF16),
                                 b_gate[l].reshape(1, 2 * D), tm=256)

        wq = _pad_cols(w_uq[l].reshape(Q_LORA, H, QK_NOPE + QK_ROPE).transpose(1, 0, 2).reshape(H * Q_LORA, -1), LANE)
        wq = wq.reshape(H, Q_LORA, LANE).astype(BF16)
        wkv = w_ukv[l].reshape(KV_LORA, H, QK_NOPE + V_HEAD)
        wk = _pad_cols(wkv[:, :, :QK_NOPE].transpose(1, 0, 2).reshape(H * KV_LORA, QK_NOPE), LANE)
        wk = wk.reshape(H, KV_LORA, LANE).astype(BF16)
        wvt = wkv[:, :, QK_NOPE:].reshape(KV_LORA, H * V_HEAD).T.astype(BF16)
        gqh = _pad_cols(jnp.concatenate([qn_norm_g[l], qr_norm_g[l]]).reshape(1, -1), LANE)
        gkn = _pad_cols(kn_norm_g[l].reshape(1, -1), LANE)
        gkr = _pad_cols(jnp.concatenate([jnp.zeros((QK_NOPE,), F32), kr_norm_g[l]]).reshape(1, -1), LANE)
        q, k, vt = _mla_prep(mla, pos, row(q_a_norm_g[l]), row(kv_a_norm_g[l]), wq, wk, wvt, gqh, gkn, gkr, freq,
                             B, S, tm=512)
        oat = _attn(q, k, vt, tq=256, tk=128)

        a2p = _pad_rows(a2[l], LANE).astype(BF16)
        w2p = jnp.pad(w2[l], ((0, 0), (0, LANE - DECAY_LORA), (0, 0))).astype(BF16)
        r_, k_, v_, kk_, bv_, bonus, lw_f, lw_b, g_f, g_b = _rwkv_prep(
            rw, _rw_layout(shift_mu[l]), row(a0[l]), a2p, w0[l], w2p, g2[l].astype(BF16), row(k_k[l]),
            row(k_a[l]), row(r_k[l]), bo, S, tm=256)
        o_f, o_b = _wkv(r_, k_, v_, kk_, bv_, lw_f, lw_b, B, S)

        wo = w_o[l].astype(BF16)
        x2 = _mix(o_f, o_b, bonus, g_f, g_b, oat, gates, x2, bo, row(ln_x_g[l]), row(ln_x_b[l]), wo[:H * V_HEAD],
                  wo[H * V_HEAD:], w_merge[l].astype(BF16), B, S, tm=256)

        x2 = _ffn(x2, row(norm_ffn_g[l]), w_ffn_gate[l].astype(BF16), w_ffn_up[l].astype(BF16), ffn_conv_w[l],
                  ffn_conv_b[l].reshape(1, -1), w_ffn_down[l].astype(BF16), S, tm=512, th=1408)
    return x2.reshape(B, S, D)
```

```python
import functools
import math

import jax
import jax.numpy as jnp
from jax import lax
from jax.experimental import pallas as pl
from jax.experimental.pallas import tpu as pltpu

F32 = jnp.float32
BF16 = jnp.bfloat16

D_MODEL = 1024
MLA_HEADS = 8
QK_NOPE = 64
QK_ROPE = 32
V_HEAD = 64
Q_LORA = 256
KV_LORA = 128
ROPE_THETA = 10000.0
RWKV_HEADS = 8
RWKV_HEAD = 64
RWKV_WIDTH = RWKV_HEADS * RWKV_HEAD
DECAY_LORA = 64
A_LORA = 64
GATE_LORA = 128
GN_EPS = 64e-5
FFN_HIDDEN = 2816
NORM_EPS = 1e-6

LANE = 128
MXU_COLS = 256
MLA_W = 512
RW_W = 3 * RWKV_WIDTH + 5 * LANE
VMEM_LIMIT = 56 * 1024 * 1024

CHUNK = 64
GROUP = 4
GW = GROUP * RWKV_HEAD
assert CHUNK == RWKV_HEAD


def _mm(a, b):
    return jnp.dot(a, b, preferred_element_type=F32)


def _mm_nt(a, b):
    return lax.dot_general(a, b, (((1,), (1,)), ((), ())), preferred_element_type=F32)


def _mm_tn(a, b):
    return lax.dot_general(a, b, (((0,), (0,)), ((), ())), preferred_element_type=F32)


def _split(x):
    hi = x.astype(BF16)
    lo = (x - hi.astype(F32)).astype(BF16)
    return hi, lo


def _mm_hl(x, b01):
    hi, lo = _split(x)
    return _mm(hi, b01) + _mm(lo, b01)


def _sigmoid(x):
    return 1.0 / (1.0 + jnp.exp(-x))


def _rms(x, g, eps=NORM_EPS):
    return x * lax.rsqrt(jnp.mean(x * x, axis=-1, keepdims=True) + eps) * g


def _const_spec(shape):
    nd = len(shape)
    return pl.BlockSpec(shape, lambda *_: (0,) * nd, pipeline_mode=pl.Buffered(1))


def _params(sem):
    return pltpu.CompilerParams(dimension_semantics=sem, vmem_limit_bytes=VMEM_LIMIT)


def _inproj_kernel(x_ref, g_ref, wm_ref, wr_ref, wg_ref, bg_ref, mla_ref, rw_ref, gate_ref):
    h = _rms(x_ref[...], g_ref[...]).astype(BF16)
    mla_ref[...] = _mm(h, wm_ref[...])
    rw_ref[...] = _mm(h, wr_ref[...])
    gate_ref[...] = _sigmoid(_mm(h, wg_ref[...]) + bg_ref[...]).astype(gate_ref.dtype)


def _inproj(x2, g, wm, wr, wg, bg, tm):
    T = x2.shape[0]
    row = lambda w: pl.BlockSpec((tm, w), lambda i: (i, 0))
    return pl.pallas_call(
        _inproj_kernel,
        grid=(T // tm,),
        in_specs=[row(D_MODEL), _const_spec(g.shape), _const_spec(wm.shape), _const_spec(wr.shape),
                  _const_spec(wg.shape), _const_spec(bg.shape)],
        out_specs=[row(MLA_W), row(RW_W), row(2 * D_MODEL)],
        out_shape=[jax.ShapeDtypeStruct((T, MLA_W), F32), jax.ShapeDtypeStruct((T, RW_W), F32),
                   jax.ShapeDtypeStruct((T, 2 * D_MODEL), BF16)],
        compiler_params=_params(("parallel",)),
        name="inproj",
    )(x2, g, wm, wr, wg, bg)


def _rotary(t, cos, sin, lane):
    first = (lane >= QK_NOPE) & (lane < QK_NOPE + QK_ROPE // 2)
    second = (lane >= QK_NOPE + QK_ROPE // 2) & (lane < QK_NOPE + QK_ROPE)
    up = pltpu.roll(t, LANE - QK_ROPE // 2, axis=1)
    dn = pltpu.roll(t, QK_ROPE // 2, axis=1)
    rot = jnp.where(first, -up, jnp.where(second, dn, 0.0))
    return t * cos + rot * sin


def _mla_prep_kernel(mla_ref, pos_ref, gq_ref, gkv_ref, wq_ref, wk_ref, wvt_ref, gqh_ref, gkn_ref,
                     gkr_ref, freq_ref, q_ref, k_ref, vt_ref):
    tm = mla_ref.shape[0]
    lane = lax.broadcasted_iota(jnp.int32, (tm, LANE), 1)
    nope = lane < QK_NOPE
    ang = pos_ref[...].astype(F32) * freq_ref[...]
    cos, sin = jnp.cos(ang), jnp.sin(ang)
    cq = _rms(mla_ref[:, 0:Q_LORA], gq_ref[...]).astype(BF16)
    ckv = _rms(mla_ref[:, Q_LORA:Q_LORA + KV_LORA], gkv_ref[...]).astype(BF16)
    kr = mla_ref[:, Q_LORA + KV_LORA:MLA_W]
    kr = kr * lax.rsqrt(jnp.sum(kr * kr, axis=-1, keepdims=True) / QK_ROPE + NORM_EPS) * gkr_ref[...]
    kr = _rotary(kr, cos, sin, lane)
    scale = math.log2(math.e) / math.sqrt(QK_NOPE + QK_ROPE)
    for h in range(MLA_HEADS):
        q = _mm(cq, wq_ref[h])
        q2 = q * q
        ss_n = jnp.sum(jnp.where(nope, q2, 0.0), axis=-1, keepdims=True)
        ss_r = jnp.sum(jnp.where(nope, 0.0, q2), axis=-1, keepdims=True)
        inv = jnp.where(nope, lax.rsqrt(ss_n / QK_NOPE + NORM_EPS), lax.rsqrt(ss_r / QK_ROPE + NORM_EPS))
        q = _rotary(q * inv * gqh_ref[...], cos, sin, lane)
        q_ref[0, h] = (q * scale).astype(BF16)
        kn = _mm(ckv, wk_ref[h])
        kn = kn * lax.rsqrt(jnp.sum(kn * kn, axis=-1, keepdims=True) / QK_NOPE + NORM_EPS) * gkn_ref[...]
        k_ref[0, h] = (kn + kr).astype(BF16)
    vt_ref[0] = _mm_nt(wvt_ref[...], ckv).astype(BF16)


def _mla_prep(mla, pos, gq, gkv, wq, wk, wvt, gqh, gkn, gkr, freq, B, S, tm):
    nj = S // tm
    H = MLA_HEADS
    consts = [gq, gkv, wq, wk, wvt, gqh, gkn, gkr, freq]
    return pl.pallas_call(
        _mla_prep_kernel,
        grid=(B, nj),
        in_specs=[pl.BlockSpec((tm, MLA_W), lambda b, j: (b * nj + j, 0)),
                  pl.BlockSpec((tm, 1), lambda b, j: (b * nj + j, 0))] + [_const_spec(c.shape) for c in consts],
        out_specs=[pl.BlockSpec((1, H, tm, LANE), lambda b, j: (b, 0, j, 0)),
                   pl.BlockSpec((1, H, tm, LANE), lambda b, j: (b, 0, j, 0)),
                   pl.BlockSpec((1, H * V_HEAD, tm), lambda b, j: (b, 0, j))],
        out_shape=[jax.ShapeDtypeStruct((B, H, S, LANE), BF16), jax.ShapeDtypeStruct((B, H, S, LANE), BF16),
                   jax.ShapeDtypeStruct((B, H * V_HEAD, S), BF16)],
        compiler_params=_params(("parallel", "parallel")),
        name="mla_prep",
    )(mla, pos, *consts)


def _attn_kernel(q_ref, k_ref, vt_ref, ot_ref, *, tk):
    S = k_ref.shape[2]
    subs = [q_ref[0, 0, i:i + MXU_COLS, :] for i in range(0, q_ref.shape[2], MXU_COLS)]
    nblk = S // tk
    ahead = 5

    def scores_for(j):
        kb = k_ref[0, 0, j * tk:(j + 1) * tk, :]
        return [_mm_nt(kb, q) for q in subs]

    m = [jnp.full((1, MXU_COLS), -jnp.inf, F32) for _ in subs]
    l = [jnp.zeros((1, MXU_COLS), F32) for _ in subs]
    acc = [jnp.zeros((V_HEAD, MXU_COLS), F32) for _ in subs]
    scores = [scores_for(j) for j in range(min(ahead, nblk))]
    for j in range(nblk):
        if j + ahead < nblk:
            scores.append(scores_for(j + ahead))
        vb = vt_ref[0, :, j * tk:(j + 1) * tk]
        for i, s in enumerate(scores[j]):
            m_new = jnp.maximum(m[i], jnp.max(s, axis=0, keepdims=True))
            p = jnp.exp2(s - m_new)
            alpha = jnp.exp2(m[i] - m_new)
            l[i] = alpha * l[i] + jnp.sum(p, axis=0, keepdims=True)
            acc[i] = alpha * acc[i] + _mm(vb, p.astype(BF16))
            m[i] = m_new
        scores[j] = None
    for i in range(len(subs)):
        ot_ref[0, :, i * MXU_COLS:(i + 1) * MXU_COLS] = (acc[i] / l[i]).astype(BF16)


def _attn(q, k, vt, tq, tk):
    B, H, S, _ = q.shape
    return pl.pallas_call(
        functools.partial(_attn_kernel, tk=tk),
        grid=(B, H, S // tq),
        in_specs=[pl.BlockSpec((1, 1, tq, LANE), lambda b, h, i: (b, h, i, 0)),
                  pl.BlockSpec((1, 1, S, LANE), lambda b, h, i: (b, h, 0, 0)),
                  pl.BlockSpec((1, V_HEAD, S), lambda b, h, i: (b, h, 0))],
        out_specs=pl.BlockSpec((1, V_HEAD, tq), lambda b, h, i: (b, h, i)),
        out_shape=jax.ShapeDtypeStruct((B, H * V_HEAD, S), BF16),
        compiler_params=_params(("parallel", "parallel", "parallel")),
        name="attn",
    )(q, k, vt)


SUBLANE = 8


def _shifted(p, before, after):
    tm = p.shape[0]
    row = lax.broadcasted_iota(jnp.int32, p.shape, 0)
    prev = jnp.where(row == 0, before, pltpu.roll(p, 1, axis=0))
    nxt = jnp.where(row == tm - 1, after, pltpu.roll(p, tm - 1, axis=0))
    return prev, nxt


def _halo_specs(tm, width, T):
    per = tm // SUBLANE
    before = pl.BlockSpec((SUBLANE, width), lambda i: (jnp.maximum(i * per - 1, 0), 0))
    after = pl.BlockSpec((SUBLANE, width), lambda i: (jnp.minimum((i + 1) * per, T // SUBLANE - 1), 0))
    return [before, after]


def _halo_rows(before_ref, after_ref, tiles_per_seq):
    j = pl.program_id(0) % tiles_per_seq
    before = jnp.where(j == 0, 0.0, before_ref[SUBLANE - 1:SUBLANE, :])
    after = jnp.where(j == tiles_per_seq - 1, 0.0, after_ref[0:1, :])
    return before, after


def _rwkv_prep_kernel(rw_ref, before_ref, after_ref, mu_ref, a0_ref, a2_ref, w0_ref, w2_ref, g2_ref, kk_ref, ka_ref,
                      rk_ref, bo_ref, r_out, k_out, v_out, kk_out, bv_out, bonus_out, lwf_out, lwb_out, gf_out,
                      gb_out, *, tiles_per_seq):
    p = rw_ref[...]
    prev, nxt = _shifted(p, *_halo_rows(before_ref, after_ref, tiles_per_seq))
    u = p + mu_ref[0:1, :] * (prev - p) + mu_ref[1:2, :] * (nxt - p)
    W = RWKV_WIDTH
    r, k, v = u[:, 0:W], u[:, W:2 * W], u[:, 2 * W:3 * W]
    a_lo = u[:, 3 * W:3 * W + LANE]
    a = _sigmoid(a0_ref[...] + _mm(a_lo.astype(BF16), a2_ref[...]))
    bo = bo_ref[...]
    kkf = k * kk_ref[...]
    kkn = kkf / jnp.maximum(jnp.sqrt(_mm_hl(kkf * kkf, bo)), 1e-12)
    k = k * (1.0 + (a - 1.0) * ka_ref[...])
    r_out[...] = r
    k_out[...] = k
    v_out[...] = v
    kk_out[...] = kkn
    bv_out[...] = kkn * a
    bonus_out[...] = _mm_hl(r * k * rk_ref[...], bo) * v
    for d, (lw_out, g_out) in enumerate(((lwf_out, gf_out), (lwb_out, gb_out))):
        dlo = u[:, 3 * W + (1 + d) * LANE:3 * W + (2 + d) * LANE]
        glo = u[:, 3 * W + (3 + d) * LANE:3 * W + (4 + d) * LANE]
        z = -(w0_ref[d:d + 1, :] + _mm(jnp.tanh(dlo).astype(BF16), w2_ref[d]))
        softplus = jnp.maximum(z, 0.0) + jnp.log(1.0 + jnp.exp(-jnp.abs(z)))
        lw_out[...] = -jnp.exp(-softplus - 0.5)
        g_out[...] = _mm(_sigmoid(glo).astype(BF16), g2_ref[d])


def _rwkv_prep(rw, mu, a0, a2, w0, w2, g2, k_k, k_a, r_k, bo, S, tm):
    T = rw.shape[0]
    W = RWKV_WIDTH
    consts = [mu, a0, a2, w0, w2, g2, k_k, k_a, r_k, bo]
    row = pl.BlockSpec((tm, W), lambda i: (i, 0))
    one = jax.ShapeDtypeStruct((T, W), F32)
    return pl.pallas_call(
        functools.partial(_rwkv_prep_kernel, tiles_per_seq=S // tm),
        grid=(T // tm,),
        in_specs=[pl.BlockSpec((tm, RW_W), lambda i: (i, 0))] + _halo_specs(tm, RW_W, T)
        + [_const_spec(c.shape) for c in consts],
        out_specs=[row] * 10,
        out_shape=[one] * 10,
        compiler_params=_params(("parallel",)),
        name="rwkv_prep",
    )(rw, rw, rw, *consts)


def _wkv_chain(r, k, v, kk, bv, lw, g, reverse):
    C = CHUNK
    sgn = -1 if reverse else 1
    rc = lax.broadcasted_iota(jnp.int32, (C, C), 0)
    cc = lax.broadcasted_iota(jnp.int32, (C, C), 1)
    tri = jnp.where((rc - cc) * sgn >= 0, 1.0, 0.0).astype(BF16)
    lw_hi, lw_lo = _split(lw)
    cum = _mm(tri, lw_hi) + _mm(tri, lw_lo)
    total = jnp.sum(lw, axis=0, keepdims=True)
    e_neg = jnp.exp(-cum)
    e_end = jnp.exp(total - cum)

    row = lax.broadcasted_iota(jnp.int32, (C, GW), 0)
    lane = lax.broadcasted_iota(jnp.int32, (C, GW), 1)
    lane_head = lane // RWKV_HEAD
    ahead = (row - lane % RWKV_HEAD) * sgn
    strict = ahead > 0
    incl = ahead >= 0

    def diag(x):
        xb = x.astype(BF16)
        return jnp.concatenate([jnp.where(lane_head == h, xb, jnp.zeros_like(xb)) for h in range(GROUP)], axis=0)

    def undiag(x):
        return sum(jnp.where(lane_head == h, x[h * C:(h + 1) * C], 0.0) for h in range(GROUP))

    a_t = (-kk * jnp.exp(cum - lw)).astype(BF16)
    r_t = r * jnp.exp(cum)
    dg_v = diag(v)

    yield
    pm = _mm_nt(jnp.concatenate([a_t, r_t.astype(BF16)], axis=0),
                jnp.concatenate([diag(bv * e_neg), diag(k * e_neg)], axis=0))
    n_pow = jnp.where(strict, pm[0:C, 0:GW], 0.0)
    m_ak = jnp.where(strict, pm[0:C, GW:2 * GW], 0.0)
    m_rb = jnp.where(incl, pm[C:2 * C, 0:GW], 0.0).astype(BF16)
    m_rk = jnp.where(incl, pm[C:2 * C, GW:2 * GW], 0.0)
    yield
    wo = _mm(jnp.concatenate([m_ak, m_rk], axis=0).astype(BF16), dg_v)
    w1, o0 = wo[0:C], wo[C:2 * C]

    levels = int(math.log2(C))
    tinv = jnp.where(ahead == 0, 1.0, n_pow)
    for lvl in range(levels):
        yield
        lhs = [] if lvl == levels - 1 else [n_pow]
        lhs += [] if lvl == 0 else [tinv]
        res = _mm(jnp.concatenate(lhs, axis=0).astype(BF16), diag(n_pow))
        if lvl > 0:
            tinv = tinv + res[(len(lhs) - 1) * C:len(lhs) * C]
        n_pow = res[0:C]

    yield
    au = _mm(tinv.astype(BF16), jnp.concatenate([diag(a_t), diag(w1)], axis=1))
    a2, u0 = au[:, 0:GW], au[:, GW:2 * GW]
    yield
    ro = _mm(m_rb, jnp.concatenate([diag(a2), diag(u0)], axis=1))
    r2 = r_t + ro[:, 0:GW]
    o0 = o0 + ro[:, GW:2 * GW]
    yield
    ends = jnp.concatenate([bv * e_end, k * e_end], axis=0).astype(BF16)
    vals = jnp.concatenate([jnp.concatenate([a2, u0], axis=1),
                            jnp.concatenate([jnp.zeros_like(v), v], axis=1)], axis=0).astype(BF16)
    pp = _mm_tn(ends, vals)
    phi, psi = undiag(pp[:, 0:GW]), undiag(pp[:, GW:2 * GW])
    yield
    res = _mm(jnp.concatenate([r2, phi], axis=0).astype(BF16), diag(g))
    on_diag = jnp.where(ahead == 0, jnp.exp(total), 0.0)
    p_end = sum(jnp.where(lane_head == h, jnp.sum(jnp.where(lane_head == h, on_diag, 0.0), axis=1, keepdims=True), 0.0)
                for h in range(GROUP))
    yield res[0:C] + o0, p_end * g + res[C:2 * C] + psi


def _wkv_kernel(rf, kf, vf, kkf, bvf, lwf, rb, kb, vb, kkb, bvb, lwb, of_ref, ob_ref, g_ref):
    @pl.when(pl.program_id(1) == 0)
    def _():
        g_ref[...] = jnp.zeros_like(g_ref)

    dirs = ((rf, kf, vf, kkf, bvf, lwf, of_ref), (rb, kb, vb, kkb, bvb, lwb, ob_ref))
    chains = []
    for bi in range(of_ref.shape[0]):
        for d, (*ins, o_ref) in enumerate(dirs):
            for gidx in range(RWKV_WIDTH // GW):
                sl = slice(gidx * GW, (gidx + 1) * GW)
                gen = _wkv_chain(*(ref[bi, :, sl] for ref in ins), g_ref[bi, d, gidx], reverse=d == 1)
                chains.append((gen, o_ref, bi, sl, d, gidx))
    while chains:
        for chain in list(chains):
            gen, o_ref, bi, sl, d, gidx = chain
            out = next(gen)
            if out is not None:
                o_ref[bi, :, sl], g_ref[bi, d, gidx] = out
                chains.remove(chain)


def _wkv(r, k, v, kk, bv, lwf, lwb, B, S, nb):
    T, W = r.shape
    C = CHUNK
    nc = S // C
    fwd = pl.BlockSpec((nb, C, W), lambda b, c: (b, c, 0))
    bwd = pl.BlockSpec((nb, C, W), lambda b, c: (b, nc - 1 - c, 0))
    out = jax.ShapeDtypeStruct((B, S, W), F32)
    r, k, v, kk, bv, lwf, lwb = (a.reshape(B, S, W) for a in (r, k, v, kk, bv, lwf, lwb))
    o_f, o_b = pl.pallas_call(
        _wkv_kernel,
        grid=(B // nb, nc),
        in_specs=[fwd] * 6 + [bwd] * 6,
        out_specs=[fwd, bwd],
        out_shape=[out, out],
        scratch_shapes=[pltpu.VMEM((nb, 2, W // GW, RWKV_HEAD, GW), F32)],
        compiler_params=_params(("parallel", "arbitrary")),
        name="wkv",
    )(r, k, v, kk, bv, lwf, r, k, v, kk, bv, lwb)
    return o_f.reshape(T, W), o_b.reshape(T, W)


def _mix_kernel(of_ref, ob_ref, bonus_ref, gf_ref, gb_ref, oat_ref, gate_ref, x_ref, bo_ref, lng_ref, lnb_ref,
                woa_ref, wob_ref, wm_ref, out_ref):
    bo = bo_ref[...]
    bonus = bonus_ref[...]
    ob = None
    for o_ref, g_ref in ((of_ref, gf_ref), (ob_ref, gb_ref)):
        o = o_ref[...]
        mu = _mm(o.astype(BF16), bo) / RWKV_HEAD
        dl = o - mu
        var = _mm((dl * dl).astype(BF16), bo) / RWKV_HEAD
        y = dl * lax.rsqrt(var + GN_EPS) * lng_ref[...] + lnb_ref[...] + bonus
        y = y * g_ref[...]
        ob = y if ob is None else ob + y
    y_b = _mm(ob.astype(BF16), wob_ref[...])
    y_a = _mm_tn(oat_ref[0], woa_ref[...])
    mixed = gate_ref[:, 0:D_MODEL] * y_a + gate_ref[:, D_MODEL:2 * D_MODEL] * y_b
    out_ref[...] = x_ref[...] + _mm(mixed.astype(BF16), wm_ref[...])


def _mix(o_f, o_b, bonus, g_f, g_b, oat, gates, x2, bo, lng, lnb, woa, wob, wm, B, S, tm):
    T = x2.shape[0]
    W = RWKV_WIDTH
    nj = S // tm
    consts = [bo, lng, lnb, woa, wob, wm]
    tok = pl.BlockSpec((tm, W), lambda b, j: (b * nj + j, 0))
    return pl.pallas_call(
        _mix_kernel,
        grid=(B, nj),
        in_specs=[tok, tok, tok, tok, tok,
                  pl.BlockSpec((1, MLA_HEADS * V_HEAD, tm), lambda b, j: (b, 0, j)),
                  pl.BlockSpec((tm, 2 * D_MODEL), lambda b, j: (b * nj + j, 0)),
                  pl.BlockSpec((tm, D_MODEL), lambda b, j: (b * nj + j, 0))]
        + [_const_spec(c.shape) for c in consts],
        out_specs=pl.BlockSpec((tm, D_MODEL), lambda b, j: (b * nj + j, 0)),
        out_shape=jax.ShapeDtypeStruct((T, D_MODEL), F32),
        compiler_params=_params(("parallel", "parallel")),
        name="mix",
    )(o_f, o_b, bonus, g_f, g_b, oat, gates, x2, *consts)


def _ffn_kernel(x_ref, before_ref, after_ref, g_ref, wg_ref, wu_ref, cw_ref, cb_ref, wd_ref, out_ref, *, th,
                tiles_per_seq):
    x = x_ref[...]
    h = _rms(x, g_ref[...]).astype(BF16)
    before, after = _halo_rows(before_ref, after_ref, tiles_per_seq)
    edge = jnp.concatenate([jnp.broadcast_to(before, (SUBLANE, D_MODEL)), jnp.broadcast_to(after, (SUBLANE, D_MODEL))], axis=0)
    hh = _rms(edge, g_ref[...]).astype(BF16)
    acc = x
    for j in range(FFN_HIDDEN // th):
        sl = slice(j * th, (j + 1) * th)
        pre = _mm(h, wg_ref[:, sl])
        pre_edge = _mm(hh, wg_ref[:, sl])
        prev, nxt = _shifted(pre, pre_edge[0:1, :], pre_edge[SUBLANE:SUBLANE + 1, :])
        gp = prev * cw_ref[0:1, sl] + pre * cw_ref[1:2, sl] + nxt * cw_ref[2:3, sl] + cb_ref[:, sl]
        act = gp * _sigmoid(gp) * _mm(h, wu_ref[:, sl])
        acc = acc + _mm(act.astype(BF16), wd_ref[sl, :])
    out_ref[...] = acc


def _ffn(x1, g, wg, wu, cw, cb, wd, S, tm, th):
    T = x1.shape[0]
    consts = [g, wg, wu, cw, cb, wd]
    return pl.pallas_call(
        functools.partial(_ffn_kernel, th=th, tiles_per_seq=S // tm),
        grid=(T // tm,),
        in_specs=[pl.BlockSpec((tm, D_MODEL), lambda i: (i, 0))] + _halo_specs(tm, D_MODEL, T)
        + [_const_spec(c.shape) for c in consts],
        out_specs=pl.BlockSpec((tm, D_MODEL), lambda i: (i, 0)),
        out_shape=jax.ShapeDtypeStruct((T, D_MODEL), F32),
        compiler_params=_params(("parallel",)),
        name="ffn",
    )(x1, x1, x1, *consts)


def _pad_cols(w, width):
    return jnp.pad(w, ((0, 0), (0, width - w.shape[1])))


def _pad_rows(w, height):
    return jnp.pad(w, ((0, height - w.shape[0]), (0, 0)))


def _rw_layout(t):
    W = RWKV_WIDTH
    pad = lambda a: jnp.pad(a, [(0, 0)] * (a.ndim - 1) + [(0, LANE - a.shape[-1])])
    o = 3 * W
    return jnp.concatenate([t[..., 0:o], pad(t[..., o:o + 64]), pad(t[..., o + 64:o + 128]),
                            pad(t[..., o + 128:o + 192]), t[..., o + 192:o + 448]], axis=-1)


def kernel(x, positions, norm_mix_g, w_in, b_gate, q_a_norm_g, kv_a_norm_g, w_uq, w_ukv, qn_norm_g, qr_norm_g,
           kn_norm_g, kr_norm_g, shift_mu, w0, w2, a0, a2, g2, k_k, k_a, r_k, ln_x_g, ln_x_b, w_o, w_merge,
           norm_ffn_g, w_ffn_gate, w_ffn_up, ffn_conv_w, ffn_conv_b, w_ffn_down):
    B, S, D = x.shape
    T = B * S
    depth = norm_mix_g.shape[0]
    H = MLA_HEADS
    mla_cols = Q_LORA + KV_LORA + QK_ROPE
    rw_cols = 3 * RWKV_WIDTH + A_LORA + 2 * DECAY_LORA + 2 * GATE_LORA

    inv_freq = ROPE_THETA ** (-jnp.arange(0, QK_ROPE, 2, dtype=F32) / QK_ROPE)
    freq = jnp.zeros((1, LANE), F32).at[0, QK_NOPE:QK_NOPE + QK_ROPE].set(jnp.concatenate([inv_freq, inv_freq]))
    head_of = jnp.arange(RWKV_WIDTH) // RWKV_HEAD
    bo = (head_of[:, None] == head_of[None, :]).astype(BF16)
    pos = positions.reshape(T, 1)
    row = lambda v: v.reshape(1, -1).astype(F32)

    x2 = x.reshape(T, D)
    for l in range(depth):
        wi = w_in[l]
        w_mla = jnp.concatenate([wi[:, 0:Q_LORA + KV_LORA], jnp.zeros((D, QK_NOPE), F32),
                                 wi[:, Q_LORA + KV_LORA:mla_cols], jnp.zeros((D, LANE - QK_NOPE - QK_ROPE), F32)], axis=1)
        w_rw = _rw_layout(wi[:, mla_cols:mla_cols + rw_cols])
        w_gate = wi[:, mla_cols + rw_cols:]
        mla, rw, gates = _inproj(x2, row(norm_mix_g[l]), w_mla.astype(BF16), w_rw.astype(BF16), w_gate.astype(BF16),
                                 b_gate[l].reshape(1, 2 * D), tm=256)

        wq = _pad_cols(w_uq[l].reshape(Q_LORA, H, QK_NOPE + QK_ROPE).transpose(1, 0, 2).reshape(H * Q_LORA, -1), LANE)
        wq = wq.reshape(H, Q_LORA, LANE).astype(BF16)
        wkv = w_ukv[l].reshape(KV_LORA, H, QK_NOPE + V_HEAD)
        wk = _pad_cols(wkv[:, :, :QK_NOPE].transpose(1, 0, 2).reshape(H * KV_LORA, QK_NOPE), LANE)
        wk = wk.reshape(H, KV_LORA, LANE).astype(BF16)
        wvt = wkv[:, :, QK_NOPE:].reshape(KV_LORA, H * V_HEAD).T.astype(BF16)
        gqh = _pad_cols(jnp.concatenate([qn_norm_g[l], qr_norm_g[l]]).reshape(1, -1), LANE)
        gkn = _pad_cols(kn_norm_g[l].reshape(1, -1), LANE)
        gkr = _pad_cols(jnp.concatenate([jnp.zeros((QK_NOPE,), F32), kr_norm_g[l]]).reshape(1, -1), LANE)
        q, k, vt = _mla_prep(mla, pos, row(q_a_norm_g[l]), row(kv_a_norm_g[l]), wq, wk, wvt, gqh, gkn, gkr, freq,
                             B, S, tm=512)
        oat = _attn(q, k, vt, tq=512, tk=256)

        a2p = _pad_rows(a2[l], LANE).astype(BF16)
        w2p = jnp.pad(w2[l], ((0, 0), (0, LANE - DECAY_LORA), (0, 0))).astype(BF16)
        r_, k_, v_, kk_, bv_, bonus, lw_f, lw_b, g_f, g_b = _rwkv_prep(
            rw, _rw_layout(shift_mu[l]), row(a0[l]), a2p, w0[l], w2p, g2[l].astype(BF16), row(k_k[l]),
            row(k_a[l]), row(r_k[l]), bo, S, tm=256)
        o_f, o_b = _wkv(r_, k_, v_, kk_, bv_, lw_f, lw_b, B, S, nb=4)

        wo = w_o[l].astype(BF16)
        x2 = _mix(o_f, o_b, bonus, g_f, g_b, oat, gates, x2, bo, row(ln_x_g[l]), row(ln_x_b[l]), wo[:H * V_HEAD],
                  wo[H * V_HEAD:], w_merge[l].astype(BF16), B, S, tm=256)

        x2 = _ffn(x2, row(norm_ffn_g[l]), w_ffn_gate[l].astype(BF16), w_ffn_up[l].astype(BF16), ffn_conv_w[l],
                  ffn_conv_b[l].reshape(1, -1), w_ffn_down[l].astype(BF16), S, tm=512, th=1408)
    return x2.reshape(B, S, D)
```

```python
import functools
import math

import jax
import jax.numpy as jnp
from jax import lax
from jax.experimental import pallas as pl
from jax.experimental.pallas import tpu as pltpu

F32 = jnp.float32
BF16 = jnp.bfloat16

D_MODEL = 1024
MLA_HEADS = 8
QK_NOPE = 64
QK_ROPE = 32
V_HEAD = 64
Q_LORA = 256
KV_LORA = 128
ROPE_THETA = 10000.0
RWKV_HEADS = 8
RWKV_HEAD = 64
RWKV_WIDTH = RWKV_HEADS * RWKV_HEAD
DECAY_LORA = 64
A_LORA = 64
GATE_LORA = 128
GN_EPS = 64e-5
FFN_HIDDEN = 2816
NORM_EPS = 1e-6

LANE = 128
SUBLANE = 8
MXU_COLS = 256
VMEM_LIMIT = 56 * 1024 * 1024

V_ROWS = V_HEAD + 16
MLA_W = 512
RW_W = 3 * RWKV_WIDTH + 5 * LANE

CHUNK = 64
GROUP = 4
GW = GROUP * RWKV_HEAD
assert CHUNK == RWKV_HEAD

TM_FRONT = 256
GATE_CHUNK = MXU_COLS
TQ_ATTN = 512
TK_ATTN = 256
ATTN_AHEAD = 5
NB_WKV = 4
TM_MIX = 256
TM_FFN = 512
TH_FFN = 1408


def _mm(a, b):
    return jnp.dot(a, b, preferred_element_type=F32)


def _mm_nt(a, b):
    return lax.dot_general(a, b, (((1,), (1,)), ((), ())), preferred_element_type=F32)


def _mm_tn(a, b):
    return lax.dot_general(a, b, (((0,), (0,)), ((), ())), preferred_element_type=F32)


def _split(x):
    hi = x.astype(BF16)
    lo = (x - hi.astype(F32)).astype(BF16)
    return hi, lo


def _head_sums(x, bo):
    xb = x.astype(BF16)
    return jnp.concatenate([_mm(xb[:, c:c + GW], bo) for c in range(0, x.shape[1], GW)], axis=1)


def _sigmoid(x):
    return 0.5 * jnp.tanh(0.5 * x) + 0.5


def _rms(x, g, eps=NORM_EPS):
    return x * lax.rsqrt(jnp.mean(x * x, axis=-1, keepdims=True) + eps) * g


def _const_spec(shape):
    nd = len(shape)
    return pl.BlockSpec(shape, lambda *_: (0,) * nd, pipeline_mode=pl.Buffered(1))


def _params(sem):
    return pltpu.CompilerParams(dimension_semantics=sem, vmem_limit_bytes=VMEM_LIMIT)


def _run_in_turn(streams):
    streams = list(streams)
    while streams:
        for st in list(streams):
            if next(st, "done") == "done":
                streams.remove(st)


def _shifted(p, before, after):
    tm = p.shape[0]
    row = lax.broadcasted_iota(jnp.int32, p.shape, 0)
    prev = jnp.where(row == 0, before, pltpu.roll(p, 1, axis=0))
    nxt = jnp.where(row == tm - 1, after, pltpu.roll(p, tm - 1, axis=0))
    return prev, nxt


def _halo_specs(tm, width, T):
    per = tm // SUBLANE
    before = pl.BlockSpec((SUBLANE, width), lambda i: (jnp.maximum(i * per - 1, 0), 0))
    after = pl.BlockSpec((SUBLANE, width), lambda i: (jnp.minimum((i + 1) * per, T // SUBLANE - 1), 0))
    return [before, after]


def _with_neighbours(x, before_ref, after_ref, tiles_per_seq):
    j = pl.program_id(0) % tiles_per_seq
    before = jnp.where(j == 0, 0.0, before_ref[SUBLANE - 1:SUBLANE, :])
    after = jnp.where(j == tiles_per_seq - 1, 0.0, after_ref[0:1, :])
    width = x.shape[1]
    return jnp.concatenate([x, jnp.broadcast_to(before, (SUBLANE, width)), jnp.broadcast_to(after, (SUBLANE, width))],
                           axis=0)


def _rotary(t, cos, sin_up, sin_dn):
    up = pltpu.roll(t, LANE - QK_ROPE // 2, axis=1)
    dn = pltpu.roll(t, QK_ROPE // 2, axis=1)
    return t * cos + up * sin_up + dn * sin_dn


def _front_kernel(x_ref, xb_ref, xa_ref, pos_ref, gmix_ref, wm_ref, wr_ref, wg_ref, bg_ref,
                  gq_ref, gkv_ref, wq_ref, wk_ref, wvt_ref, gqh_ref, gkn_ref, gkr_ref, freq_ref, place_ref, base_ref,
                  one_ref, seg_ref, invn_ref,
                  mu_ref, a0_ref, a2_ref, w0_ref, w2_ref, g2_ref, kk_ref, ka_ref, rk_ref, bo_ref,
                  gate_out, q_out, k_out, vt_out,
                  r_out, kw_out, v_out, kk_out, bv_out, bonus_out, lwf_out, lwb_out, gf_out, gb_out, *, tiles_per_seq):
    tm = x_ref.shape[0]
    gmix = gmix_ref[...]
    h_all = _rms(_with_neighbours(x_ref[...], xb_ref, xa_ref, tiles_per_seq), gmix).astype(BF16)
    h = h_all[0:tm]

    def gates():
        for c in range(0, 2 * D_MODEL, GATE_CHUNK):
            sl = slice(c, c + GATE_CHUNK)
            gate_out[:, sl] = _sigmoid(_mm(h, wg_ref[:, sl]) + bg_ref[:, sl]).astype(gate_out.dtype)
            yield

    def rwkv():
        p_all = _mm(h_all, wr_ref[...])
        yield
        p = p_all[0:tm]
        prev, nxt = _shifted(p, p_all[tm:tm + 1, :], p_all[tm + SUBLANE:tm + SUBLANE + 1, :])
        u = p * mu_ref[2:3, :] + prev * mu_ref[0:1, :] + nxt * mu_ref[1:2, :]
        W = RWKV_WIDTH
        r, k, v = u[:, 0:W], u[:, W:2 * W], u[:, 2 * W:3 * W]
        r_out[...] = r
        v_out[...] = v
        yield
        a = _sigmoid(a0_ref[...] + _mm(u[:, 3 * W:3 * W + LANE].astype(BF16), a2_ref[...]))
        bo = bo_ref[...]
        kkf = k * kk_ref[...]
        ss = _head_sums(kkf * kkf, bo)
        yield
        kkn = kkf * jnp.where(ss >= 1e-24, lax.rsqrt(ss), 1e12)
        k = k * (1.0 + (a - 1.0) * ka_ref[...])
        kw_out[...] = k
        kk_out[...] = kkn
        bv_out[...] = kkn * a
        yield
        bonus_out[...] = _head_sums(r * k * rk_ref[...], bo) * v
        yield
        for d, (lw_out, g_out) in enumerate(((lwf_out, gf_out), (lwb_out, gb_out))):
            dlo = u[:, 3 * W + (1 + d) * LANE:3 * W + (2 + d) * LANE]
            glo = u[:, 3 * W + (3 + d) * LANE:3 * W + (4 + d) * LANE]
            z = -(w0_ref[d:d + 1, :] + _mm(jnp.tanh(dlo).astype(BF16), w2_ref[d]))
            softplus = jnp.maximum(z, 0.0) + jnp.log(1.0 + jnp.exp(-jnp.abs(z)))
            lw_out[...] = -jnp.exp(-softplus - 0.5)
            yield
            g_out[...] = _mm(_sigmoid(glo).astype(BF16), g2_ref[d])
            yield

    def mla():
        lat = _mm(h, wm_ref[...])
        ang = freq_ref[...] * pos_ref[...].astype(F32)
        place = place_ref[...]
        cos_hl = jnp.concatenate(_split(jnp.cos(ang)), axis=0)
        sin_hl = jnp.concatenate(_split(jnp.sin(ang)), axis=0)
        yield
        cos = _mm_tn(cos_hl, jnp.concatenate([place[0], place[0]], axis=0)) + base_ref[...]
        sin_up = _mm_tn(sin_hl, jnp.concatenate([place[1], place[1]], axis=0))
        sin_dn = _mm_tn(sin_hl, jnp.concatenate([place[2], place[2]], axis=0))
        seg, invn = seg_ref[...], invn_ref[...]

        def head_norm(t, gain):
            ss = _mm((t * t).astype(BF16), seg[0:t.shape[1], 0:t.shape[1]])
            return t * lax.rsqrt(ss * invn[:, 0:t.shape[1]] + NORM_EPS) * gain

        cq = _rms(lat[:, 0:Q_LORA], gq_ref[...]).astype(BF16)
        ckv = _rms(lat[:, Q_LORA:Q_LORA + KV_LORA], gkv_ref[...]).astype(BF16)
        yield
        kr = _rotary(head_norm(lat[:, Q_LORA + KV_LORA:MLA_W], gkr_ref[...]), cos, sin_up, sin_dn)
        vt_out[0] = (_mm_nt(wvt_ref[...], ckv) + one_ref[...]).astype(BF16)
        yield
        for pair in range(MLA_HEADS // 2):
            q2 = head_norm(_mm(cq, wq_ref[pair]), gqh_ref[...])
            for i in range(2):
                q_out[0, 2 * pair + i] = _rotary(q2[:, i * LANE:(i + 1) * LANE], cos, sin_up, sin_dn).astype(BF16)
            yield
            k2 = head_norm(_mm(ckv, wk_ref[pair]), gkn_ref[...])
            for i in range(2):
                k_out[0, 2 * pair + i] = (k2[:, i * LANE:(i + 1) * LANE] + kr).astype(BF16)
            yield

    _run_in_turn([gates(), rwkv(), mla()])


def _front(x2, pos_row, consts, B, S):
    T = x2.shape[0]
    tm = TM_FRONT
    nj = S // tm
    H = MLA_HEADS
    W = RWKV_WIDTH
    tok = lambda w: pl.BlockSpec((tm, w), lambda i: (i, 0))
    heads = pl.BlockSpec((1, H, tm, LANE), lambda i: (i // nj, 0, i % nj, 0))
    one = jax.ShapeDtypeStruct((T, W), F32)
    return pl.pallas_call(
        functools.partial(_front_kernel, tiles_per_seq=nj),
        grid=(T // tm,),
        in_specs=[tok(D_MODEL)] + _halo_specs(tm, D_MODEL, T) + [pl.BlockSpec((1, tm), lambda i: (0, i))]
        + [_const_spec(c.shape) for c in consts],
        out_specs=[tok(2 * D_MODEL), heads, heads, pl.BlockSpec((1, H * V_ROWS, tm), lambda i: (i // nj, 0, i % nj))]
        + [tok(W)] * 10,
        out_shape=[jax.ShapeDtypeStruct((T, 2 * D_MODEL), BF16), jax.ShapeDtypeStruct((B, H, S, LANE), BF16),
                   jax.ShapeDtypeStruct((B, H, S, LANE), BF16), jax.ShapeDtypeStruct((B, H * V_ROWS, S), BF16)]
        + [one] * 10,
        compiler_params=_params(("parallel",)),
        name="front",
    )(x2, x2, x2, pos_row, *consts)


def _attn_kernel(q_ref, k_ref, vt_ref, ot_ref):
    S = k_ref.shape[2]
    tk = TK_ATTN
    subs = [q_ref[0, 0, i:i + MXU_COLS, :] for i in range(0, q_ref.shape[2], MXU_COLS)]
    nblk = S // tk

    def scores_for(j):
        kb = k_ref[0, 0, j * tk:(j + 1) * tk, :]
        return [_mm_nt(kb, q) for q in subs]

    m = [jnp.full((1, MXU_COLS), -jnp.inf, F32) for _ in subs]
    acc = [jnp.zeros((V_ROWS, MXU_COLS), F32) for _ in subs]
    scores = [scores_for(j) for j in range(min(ATTN_AHEAD, nblk))]
    for j in range(nblk):
        if j + ATTN_AHEAD < nblk:
            scores.append(scores_for(j + ATTN_AHEAD))
        vb = vt_ref[0, :, j * tk:(j + 1) * tk]
        for i, s in enumerate(scores[j]):
            m_new = jnp.maximum(m[i], jnp.max(s, axis=0, keepdims=True))
            p = jnp.exp2(s - m_new)
            alpha = jnp.exp2(m[i] - m_new)
            acc[i] = alpha * acc[i] + _mm(vb, p.astype(BF16))
            m[i] = m_new
        scores[j] = None
    for i in range(len(subs)):
        ot_ref[0, :, i * MXU_COLS:(i + 1) * MXU_COLS] = (acc[i][0:V_HEAD] / acc[i][V_HEAD:V_HEAD + 1]).astype(BF16)


def _attn(q, k, vt):
    B, H, S, _ = q.shape
    tq = TQ_ATTN
    return pl.pallas_call(
        _attn_kernel,
        grid=(B, H, S // tq),
        in_specs=[pl.BlockSpec((1, 1, tq, LANE), lambda b, h, i: (b, h, i, 0)),
                  pl.BlockSpec((1, 1, S, LANE), lambda b, h, i: (b, h, 0, 0)),
                  pl.BlockSpec((1, V_ROWS, S), lambda b, h, i: (b, h, 0))],
        out_specs=pl.BlockSpec((1, V_HEAD, tq), lambda b, h, i: (b, h, i)),
        out_shape=jax.ShapeDtypeStruct((B, H * V_HEAD, S), BF16),
        compiler_params=_params(("parallel", "parallel", "parallel")),
        name="attn",
    )(q, k, vt)


def _wkv_chain(r, k, v, kk, bv, lw, g, reverse):
    C = CHUNK
    sgn = -1 if reverse else 1
    rc = lax.broadcasted_iota(jnp.int32, (C, C), 0)
    cc = lax.broadcasted_iota(jnp.int32, (C, C), 1)
    tri = jnp.where((rc - cc) * sgn >= 0, 1.0, 0.0).astype(BF16)
    lw_hi, lw_lo = _split(lw)
    cum = _mm(tri, lw_hi) + _mm(tri, lw_lo)
    total = jnp.sum(lw, axis=0, keepdims=True)
    e_neg = jnp.exp(-cum)
    e_end = jnp.exp(total - cum)

    row = lax.broadcasted_iota(jnp.int32, (C, GW), 0)
    lane = lax.broadcasted_iota(jnp.int32, (C, GW), 1)
    lane_head = lane // RWKV_HEAD
    ahead = (row - lane % RWKV_HEAD) * sgn
    strict = ahead > 0
    incl = ahead >= 0

    def diag(x):
        xb = x.astype(BF16)
        return jnp.concatenate([jnp.where(lane_head == h, xb, jnp.zeros_like(xb)) for h in range(GROUP)], axis=0)

    def undiag(x):
        return sum(jnp.where(lane_head == h, x[h * C:(h + 1) * C], 0.0) for h in range(GROUP))

    a_t = (-kk * jnp.exp(cum - lw)).astype(BF16)
    r_t = r * jnp.exp(cum)
    dg_v = diag(v)

    yield
    pm = _mm_nt(jnp.concatenate([a_t, r_t.astype(BF16)], axis=0),
                jnp.concatenate([diag(bv * e_neg), diag(k * e_neg)], axis=0))
    n_pow = jnp.where(strict, pm[0:C, 0:GW], 0.0)
    m_ak = jnp.where(strict, pm[0:C, GW:2 * GW], 0.0)
    m_rb = jnp.where(incl, pm[C:2 * C, 0:GW], 0.0).astype(BF16)
    m_rk = jnp.where(incl, pm[C:2 * C, GW:2 * GW], 0.0)
    yield
    wo = _mm(jnp.concatenate([m_ak, m_rk], axis=0).astype(BF16), dg_v)
    w1, o0 = wo[0:C], wo[C:2 * C]

    levels = int(math.log2(C))
    tinv = jnp.where(ahead == 0, 1.0, n_pow)
    for lvl in range(levels):
        yield
        lhs = [] if lvl == levels - 1 else [n_pow]
        lhs += [] if lvl == 0 else [tinv]
        res = _mm(jnp.concatenate(lhs, axis=0).astype(BF16), diag(n_pow))
        if lvl > 0:
            tinv = tinv + res[(len(lhs) - 1) * C:len(lhs) * C]
        n_pow = res[0:C]

    yield
    au = _mm(tinv.astype(BF16), jnp.concatenate([diag(a_t), diag(w1)], axis=1))
    a2, u0 = au[:, 0:GW], au[:, GW:2 * GW]
    yield
    ro = _mm(m_rb, jnp.concatenate([diag(a2), diag(u0)], axis=1))
    r2 = r_t + ro[:, 0:GW]
    o0 = o0 + ro[:, GW:2 * GW]
    yield
    ends = jnp.concatenate([bv * e_end, k * e_end], axis=0).astype(BF16)
    vals = jnp.concatenate([jnp.concatenate([a2, u0], axis=1),
                            jnp.concatenate([jnp.zeros_like(v), v], axis=1)], axis=0).astype(BF16)
    pp = _mm_tn(ends, vals)
    phi, psi = undiag(pp[:, 0:GW]), undiag(pp[:, GW:2 * GW])
    yield
    res = _mm(jnp.concatenate([r2, phi], axis=0).astype(BF16), diag(g))
    on_diag = jnp.where(ahead == 0, jnp.exp(total), 0.0)
    p_end = sum(jnp.where(lane_head == h, jnp.sum(jnp.where(lane_head == h, on_diag, 0.0), axis=1, keepdims=True), 0.0)
                for h in range(GROUP))
    yield res[0:C] + o0, p_end * g + res[C:2 * C] + psi


def _wkv_kernel(rf, kf, vf, kkf, bvf, lwf, rb, kb, vb, kkb, bvb, lwb, of_ref, ob_ref, g_ref):
    @pl.when(pl.program_id(1) == 0)
    def _():
        g_ref[...] = jnp.zeros_like(g_ref)

    dirs = ((rf, kf, vf, kkf, bvf, lwf, of_ref), (rb, kb, vb, kkb, bvb, lwb, ob_ref))
    chains = []
    for bi in range(of_ref.shape[0]):
        for d, (*ins, o_ref) in enumerate(dirs):
            for gidx in range(RWKV_WIDTH // GW):
                sl = slice(gidx * GW, (gidx + 1) * GW)
                gen = _wkv_chain(*(ref[bi, :, sl] for ref in ins), g_ref[bi, d, gidx], reverse=d == 1)
                chains.append((gen, o_ref, bi, sl, d, gidx))
    while chains:
        for chain in list(chains):
            gen, o_ref, bi, sl, d, gidx = chain
            out = next(gen)
            if out is not None:
                o_ref[bi, :, sl], g_ref[bi, d, gidx] = out
                chains.remove(chain)


def _wkv(r, k, v, kk, bv, lwf, lwb, B, S):
    T, W = r.shape
    C = CHUNK
    nb = NB_WKV
    nc = S // C
    fwd = pl.BlockSpec((nb, C, W), lambda b, c: (b, c, 0))
    bwd = pl.BlockSpec((nb, C, W), lambda b, c: (b, nc - 1 - c, 0))
    out = jax.ShapeDtypeStruct((B, S, W), F32)
    r, k, v, kk, bv, lwf, lwb = (a.reshape(B, S, W) for a in (r, k, v, kk, bv, lwf, lwb))
    o_f, o_b = pl.pallas_call(
        _wkv_kernel,
        grid=(B // nb, nc),
        in_specs=[fwd] * 6 + [bwd] * 6,
        out_specs=[fwd, bwd],
        out_shape=[out, out],
        scratch_shapes=[pltpu.VMEM((nb, 2, W // GW, RWKV_HEAD, GW), F32)],
        compiler_params=_params(("parallel", "arbitrary")),
        name="wkv",
    )(r, k, v, kk, bv, lwf, r, k, v, kk, bv, lwb)
    return o_f.reshape(T, W), o_b.reshape(T, W)


def _mix_kernel(of_ref, ob_ref, bonus_ref, gf_ref, gb_ref, oat_ref, gate_ref, x_ref, bo_ref, lng_ref, lnb_ref,
                woa_ref, wob_ref, wm_ref, out_ref):
    bo = bo_ref[...]
    bonus = bonus_ref[...]
    ob = None
    for o_ref, g_ref in ((of_ref, gf_ref), (ob_ref, gb_ref)):
        o = o_ref[...]
        mu = _head_sums(o, bo) / RWKV_HEAD
        dl = o - mu
        var = _head_sums(dl * dl, bo) / RWKV_HEAD
        y = dl * lax.rsqrt(var + GN_EPS) * lng_ref[...] + lnb_ref[...] + bonus
        y = y * g_ref[...]
        ob = y if ob is None else ob + y
    y_b = _mm(ob.astype(BF16), wob_ref[...])
    y_a = _mm_tn(oat_ref[0], woa_ref[...])
    mixed = gate_ref[:, 0:D_MODEL] * y_a + gate_ref[:, D_MODEL:2 * D_MODEL] * y_b
    out_ref[...] = x_ref[...] + _mm(mixed.astype(BF16), wm_ref[...])


def _mix(o_f, o_b, bonus, g_f, g_b, oat, gates, x2, bo, lng, lnb, woa, wob, wm, B, S):
    T = x2.shape[0]
    W = RWKV_WIDTH
    tm = TM_MIX
    nj = S // tm
    consts = [bo, lng, lnb, woa, wob, wm]
    tok = pl.BlockSpec((tm, W), lambda b, j: (b * nj + j, 0))
    return pl.pallas_call(
        _mix_kernel,
        grid=(B, nj),
        in_specs=[tok, tok, tok, tok, tok,
                  pl.BlockSpec((1, MLA_HEADS * V_HEAD, tm), lambda b, j: (b, 0, j)),
                  pl.BlockSpec((tm, 2 * D_MODEL), lambda b, j: (b * nj + j, 0)),
                  pl.BlockSpec((tm, D_MODEL), lambda b, j: (b * nj + j, 0))]
        + [_const_spec(c.shape) for c in consts],
        out_specs=pl.BlockSpec((tm, D_MODEL), lambda b, j: (b * nj + j, 0)),
        out_shape=jax.ShapeDtypeStruct((T, D_MODEL), F32),
        compiler_params=_params(("parallel", "parallel")),
        name="mix",
    )(o_f, o_b, bonus, g_f, g_b, oat, gates, x2, *consts)


def _ffn_kernel(x_ref, before_ref, after_ref, g_ref, wg_ref, wu_ref, cw_ref, cb_ref, wd_ref, out_ref, *, tiles_per_seq):
    tm = x_ref.shape[0]
    x = x_ref[...]
    h_all = _rms(_with_neighbours(x, before_ref, after_ref, tiles_per_seq), g_ref[...]).astype(BF16)
    h = h_all[0:tm]
    chunks = [slice(c, c + TH_FFN) for c in range(0, FFN_HIDDEN, TH_FFN)]
    pres = [_mm(h_all, wg_ref[:, sl]) for sl in chunks]
    ups = [_mm(h, wu_ref[:, sl]) for sl in chunks]
    acc = x
    for sl, pre_all, up in zip(chunks, pres, ups):
        pre = pre_all[0:tm]
        prev, nxt = _shifted(pre, pre_all[tm:tm + 1, :], pre_all[tm + SUBLANE:tm + SUBLANE + 1, :])
        gp = prev * cw_ref[0:1, sl] + pre * cw_ref[1:2, sl] + nxt * cw_ref[2:3, sl] + cb_ref[:, sl]
        act = gp * _sigmoid(gp) * up
        acc = acc + _mm(act.astype(BF16), wd_ref[sl, :])
    out_ref[...] = acc


def _ffn(x1, g, wg, wu, cw, cb, wd, S):
    T = x1.shape[0]
    tm = TM_FFN
    consts = [g, wg, wu, cw, cb, wd]
    return pl.pallas_call(
        functools.partial(_ffn_kernel, tiles_per_seq=S // tm),
        grid=(T // tm,),
        in_specs=[pl.BlockSpec((tm, D_MODEL), lambda i: (i, 0))] + _halo_specs(tm, D_MODEL, T)
        + [_const_spec(c.shape) for c in consts],
        out_specs=pl.BlockSpec((tm, D_MODEL), lambda i: (i, 0)),
        out_shape=jax.ShapeDtypeStruct((T, D_MODEL), F32),
        compiler_params=_params(("parallel",)),
        name="ffn",
    )(x1, x1, x1, *consts)


def _pad_cols(w, width):
    return jnp.pad(w, ((0, 0), (0, width - w.shape[1])))


def _pad_rows(w, height):
    return jnp.pad(w, ((0, height - w.shape[0]), (0, 0)))


def _rw_layout(t):
    W = RWKV_WIDTH
    pad = lambda a: jnp.pad(a, [(0, 0)] * (a.ndim - 1) + [(0, LANE - a.shape[-1])])
    o = 3 * W
    return jnp.concatenate([t[..., 0:o], pad(t[..., o:o + 64]), pad(t[..., o + 64:o + 128]),
                            pad(t[..., o + 128:o + 192]), t[..., o + 192:o + 448]], axis=-1)


def kernel(x, positions, norm_mix_g, w_in, b_gate, q_a_norm_g, kv_a_norm_g, w_uq, w_ukv, qn_norm_g, qr_norm_g,
           kn_norm_g, kr_norm_g, shift_mu, w0, w2, a0, a2, g2, k_k, k_a, r_k, ln_x_g, ln_x_b, w_o, w_merge,
           norm_ffn_g, w_ffn_gate, w_ffn_up, ffn_conv_w, ffn_conv_b, w_ffn_down):
    B, S, D = x.shape
    T = B * S
    depth = norm_mix_g.shape[0]
    H = MLA_HEADS
    mla_cols = Q_LORA + KV_LORA + QK_ROPE
    rw_cols = 3 * RWKV_WIDTH + A_LORA + 2 * DECAY_LORA + 2 * GATE_LORA

    half = QK_ROPE // 2
    inv_freq = ROPE_THETA ** (-jnp.arange(0, QK_ROPE, 2, dtype=F32) / QK_ROPE)
    freq = inv_freq.reshape(half, 1)
    fr = jnp.arange(half)
    place = jnp.zeros((3, half, LANE), F32)
    place = place.at[0, fr, QK_NOPE + fr].set(1.0).at[0, fr, QK_NOPE + half + fr].set(1.0)
    place = place.at[1, fr, QK_NOPE + fr].set(-1.0)
    place = place.at[2, fr, QK_NOPE + half + fr].set(1.0)
    place = place.astype(BF16)
    lane = jnp.arange(LANE)
    base = ((lane < QK_NOPE) | (lane >= QK_NOPE + QK_ROPE)).astype(F32).reshape(1, LANE)
    lane2 = jnp.arange(2 * LANE)
    seg_of = 2 * (lane2 // LANE) + (lane2 % LANE >= QK_NOPE)
    seg = (seg_of[:, None] == seg_of[None, :]).astype(BF16)
    invn = jnp.where(lane2 % LANE < QK_NOPE, 1.0 / QK_NOPE, 1.0 / QK_ROPE).astype(F32).reshape(1, 2 * LANE)
    head_of = jnp.arange(GW) // RWKV_HEAD
    bo = (head_of[:, None] == head_of[None, :]).astype(BF16)
    pos_row = positions.reshape(1, T)
    row = lambda v: v.reshape(1, -1).astype(F32)
    scale = math.log2(math.e) / math.sqrt(QK_NOPE + QK_ROPE)

    x2 = x.reshape(T, D)
    for l in range(depth):
        wi = w_in[l]
        w_mla = jnp.concatenate([wi[:, 0:Q_LORA + KV_LORA], jnp.zeros((D, QK_NOPE), F32),
                                 wi[:, Q_LORA + KV_LORA:mla_cols], jnp.zeros((D, LANE - QK_NOPE - QK_ROPE), F32)], axis=1)
        w_rw = _rw_layout(wi[:, mla_cols:mla_cols + rw_cols])
        w_gate = wi[:, mla_cols + rw_cols:]
        wq = _pad_cols(w_uq[l].reshape(Q_LORA, H, QK_NOPE + QK_ROPE).transpose(1, 0, 2).reshape(H * Q_LORA, -1), LANE)
        wq = wq.reshape(H // 2, 2, Q_LORA, LANE).transpose(0, 2, 1, 3).reshape(H // 2, Q_LORA, 2 * LANE).astype(BF16)
        wkv = w_ukv[l].reshape(KV_LORA, H, QK_NOPE + V_HEAD)
        wk = _pad_cols(wkv[:, :, :QK_NOPE].transpose(1, 0, 2).reshape(H * KV_LORA, QK_NOPE), LANE)
        wk = wk.reshape(H // 2, 2, KV_LORA, LANE).transpose(0, 2, 1, 3).reshape(H // 2, KV_LORA, 2 * LANE).astype(BF16)
        wvt = jnp.pad(wkv[:, :, QK_NOPE:], ((0, 0), (0, 0), (0, V_ROWS - V_HEAD))).reshape(KV_LORA, H * V_ROWS).T.astype(BF16)
        one = jnp.tile((jnp.arange(V_ROWS) == V_HEAD).astype(F32), H).reshape(H * V_ROWS, 1)
        gqh = jnp.tile(_pad_cols(jnp.concatenate([qn_norm_g[l], qr_norm_g[l]]).reshape(1, -1), LANE) * scale, (1, 2))
        gkn = jnp.tile(_pad_cols(kn_norm_g[l].reshape(1, -1), LANE), (1, 2))
        gkr = _pad_cols(jnp.concatenate([jnp.zeros((QK_NOPE,), F32), kr_norm_g[l]]).reshape(1, -1), LANE)
        mu = _rw_layout(shift_mu[l])
        mu = jnp.concatenate([mu, 1.0 - mu[0:1] - mu[1:2]], axis=0)
        a2p = _pad_rows(a2[l], LANE).astype(BF16)
        w2p = jnp.pad(w2[l], ((0, 0), (0, LANE - DECAY_LORA), (0, 0))).astype(BF16)
        consts = [row(norm_mix_g[l]), w_mla.astype(BF16), w_rw.astype(BF16), w_gate.astype(BF16),
                  b_gate[l].reshape(1, 2 * D),
                  row(q_a_norm_g[l]), row(kv_a_norm_g[l]), wq, wk, wvt, gqh, gkn, gkr, freq, place, base, one, seg, invn,
                  mu, row(a0[l]), a2p, w0[l], w2p, g2[l].astype(BF16), row(k_k[l]), row(k_a[l]), row(r_k[l]), bo]

        gates, q, k, vt, r_, k_, v_, kk_, bv_, bonus, lw_f, lw_b, g_f, g_b = _front(x2, pos_row, consts, B, S)
        oat = _attn(q, k, vt)
        o_f, o_b = _wkv(r_, k_, v_, kk_, bv_, lw_f, lw_b, B, S)
        wo = w_o[l].astype(BF16)
        x2 = _mix(o_f, o_b, bonus, g_f, g_b, oat, gates, x2, bo, row(ln_x_g[l]), row(ln_x_b[l]), wo[:H * V_HEAD],
                  wo[H * V_HEAD:], w_merge[l].astype(BF16), B, S)
        x2 = _ffn(x2, row(norm_ffn_g[l]), w_ffn_gate[l].astype(BF16), w_ffn_up[l].astype(BF16), ffn_conv_w[l],
                  ffn_conv_b[l].reshape(1, -1), w_ffn_down[l].astype(BF16), S)
    return x2.reshape(B, S, D)
```

```python
import functools
import math

import jax
import jax.numpy as jnp
from jax import lax
from jax.experimental import pallas as pl
from jax.experimental.pallas import tpu as pltpu

F32 = jnp.float32
BF16 = jnp.bfloat16

D_MODEL = 1024
MLA_HEADS = 8
QK_NOPE = 64
QK_ROPE = 32
V_HEAD = 64
Q_LORA = 256
KV_LORA = 128
ROPE_THETA = 10000.0
RWKV_HEADS = 8
RWKV_HEAD = 64
RWKV_WIDTH = RWKV_HEADS * RWKV_HEAD
DECAY_LORA = 64
A_LORA = 64
GATE_LORA = 128
GN_EPS = 64e-5
FFN_HIDDEN = 2816
NORM_EPS = 1e-6

LANE = 128
SUBLANE = 8
MXU_COLS = 256
VMEM_LIMIT = 56 * 1024 * 1024

V_ROWS = V_HEAD + 16
MLA_W = 512
RW_W = 3 * RWKV_WIDTH + 5 * LANE

CHUNK = 64
GROUP = 4
GW = GROUP * RWKV_HEAD
assert CHUNK == RWKV_HEAD

_seg = lambda i: slice(i * RWKV_WIDTH, (i + 1) * RWKV_WIDTH)
SCAN_R, SCAN_K, SCAN_V, SCAN_KK, SCAN_BV = (_seg(i) for i in range(5))
SCAN_LW = (_seg(5), _seg(6))
SCAN_W = 7 * RWKV_WIDTH
POST_BONUS = _seg(0)
POST_GATE = (_seg(1), _seg(2))
POST_W = 3 * RWKV_WIDTH

TM_FRONT = 256
GATE_CHUNK = MXU_COLS
TQ_ATTN = 2048
TK_ATTN = 256
ATTN_AHEAD = 2
NB_WKV = 4
TM_MIX = 512
TM_FFN = 512
TH_FFN = 1408


def _mm(a, b):
    return jnp.dot(a, b, preferred_element_type=F32)


def _mm_nt(a, b):
    return lax.dot_general(a, b, (((1,), (1,)), ((), ())), preferred_element_type=F32)


def _mm_tn(a, b):
    return lax.dot_general(a, b, (((0,), (0,)), ((), ())), preferred_element_type=F32)


def _split(x):
    hi = x.astype(BF16)
    lo = (x - hi.astype(F32)).astype(BF16)
    return hi, lo


def _head_sums(x, bo):
    xb = x.astype(BF16)
    return jnp.concatenate([_mm(xb[:, c:c + GW], bo) for c in range(0, x.shape[1], GW)], axis=1)


def _sigmoid(x):
    return 0.5 * jnp.tanh(0.5 * x) + 0.5


def _rms(x, g, eps=NORM_EPS):
    return x * lax.rsqrt(jnp.mean(x * x, axis=-1, keepdims=True) + eps) * g


def _const_spec(shape):
    nd = len(shape)
    return pl.BlockSpec(shape, lambda *_: (0,) * nd, pipeline_mode=pl.Buffered(1))


def _params(sem):
    return pltpu.CompilerParams(dimension_semantics=sem, vmem_limit_bytes=VMEM_LIMIT)


def _run_in_turn(streams):
    streams = list(streams)
    while streams:
        for st in list(streams):
            if next(st, "done") == "done":
                streams.remove(st)


def _shifted(p, before, after):
    tm = p.shape[0]
    row = lax.broadcasted_iota(jnp.int32, p.shape, 0)
    prev = jnp.where(row == 0, before, pltpu.roll(p, 1, axis=0))
    nxt = jnp.where(row == tm - 1, after, pltpu.roll(p, tm - 1, axis=0))
    return prev, nxt


def _halo_specs(tm, width, T):
    per = tm // SUBLANE
    before = pl.BlockSpec((SUBLANE, width), lambda i: (jnp.maximum(i * per - 1, 0), 0))
    after = pl.BlockSpec((SUBLANE, width), lambda i: (jnp.minimum((i + 1) * per, T // SUBLANE - 1), 0))
    return [before, after]


def _with_neighbours(x, before_ref, after_ref, tiles_per_seq):
    j = pl.program_id(0) % tiles_per_seq
    before = jnp.where(j == 0, 0.0, before_ref[SUBLANE - 1:SUBLANE, :])
    after = jnp.where(j == tiles_per_seq - 1, 0.0, after_ref[0:1, :])
    width = x.shape[1]
    return jnp.concatenate([x, jnp.broadcast_to(before, (SUBLANE, width)), jnp.broadcast_to(after, (SUBLANE, width))],
                           axis=0)


def _rotary(t, cos, sin_up, sin_dn):
    up = pltpu.roll(t, LANE - QK_ROPE // 2, axis=1)
    dn = pltpu.roll(t, QK_ROPE // 2, axis=1)
    return t * cos + up * sin_up + dn * sin_dn


def _front_kernel(x_ref, xb_ref, xa_ref, pos_ref, gmix_ref, wm_ref, wr_ref, wg_ref, bg_ref,
                  gq_ref, gkv_ref, wq_ref, wk_ref, wvt_ref, gqh_ref, gkn_ref, gkr_ref, freq_ref, place_ref, base_ref,
                  one_ref, seg_ref, invn_ref,
                  mu_ref, a0_ref, a2_ref, w0_ref, w2_ref, g2_ref, kk_ref, ka_ref, rk_ref, bo_ref,
                  gate_out, q_out, k_out, vt_out,
                  scan_out, post_out, *, tiles_per_seq):
    tm = x_ref.shape[0]
    gmix = gmix_ref[...]
    h_all = _rms(_with_neighbours(x_ref[...], xb_ref, xa_ref, tiles_per_seq), gmix).astype(BF16)
    h = h_all[0:tm]

    def gates():
        for c in range(0, 2 * D_MODEL, GATE_CHUNK):
            sl = slice(c, c + GATE_CHUNK)
            gate_out[:, sl] = _sigmoid(_mm(h, wg_ref[:, sl]) + bg_ref[:, sl]).astype(gate_out.dtype)
            yield

    def rwkv():
        p_all = _mm(h_all, wr_ref[...])
        yield
        p = p_all[0:tm]
        prev, nxt = _shifted(p, p_all[tm:tm + 1, :], p_all[tm + SUBLANE:tm + SUBLANE + 1, :])
        u = p * mu_ref[2:3, :] + prev * mu_ref[0:1, :] + nxt * mu_ref[1:2, :]
        W = RWKV_WIDTH
        r, k, v = u[:, 0:W], u[:, W:2 * W], u[:, 2 * W:3 * W]
        scan_out[:, SCAN_R] = r
        scan_out[:, SCAN_V] = v
        yield
        a = _sigmoid(a0_ref[...] + _mm(u[:, 3 * W:3 * W + LANE].astype(BF16), a2_ref[...]))
        bo = bo_ref[...]
        kkf = k * kk_ref[...]
        ss = _head_sums(kkf * kkf, bo)
        yield
        kkn = kkf * jnp.where(ss >= 1e-24, lax.rsqrt(ss), 1e12)
        k = k * (1.0 + (a - 1.0) * ka_ref[...])
        scan_out[:, SCAN_K] = k
        scan_out[:, SCAN_KK] = kkn
        scan_out[:, SCAN_BV] = kkn * a
        yield
        post_out[:, POST_BONUS] = (_head_sums(r * k * rk_ref[...], bo) * v).astype(post_out.dtype)
        yield
        for d in range(2):
            dlo = u[:, 3 * W + (1 + d) * LANE:3 * W + (2 + d) * LANE]
            glo = u[:, 3 * W + (3 + d) * LANE:3 * W + (4 + d) * LANE]
            z = -(w0_ref[d:d + 1, :] + _mm(jnp.tanh(dlo).astype(BF16), w2_ref[d]))
            softplus = jnp.maximum(z, 0.0) + jnp.log(1.0 + jnp.exp(-jnp.abs(z)))
            scan_out[:, SCAN_LW[d]] = -jnp.exp(-softplus - 0.5)
            yield
            post_out[:, POST_GATE[d]] = _mm(_sigmoid(glo).astype(BF16), g2_ref[d]).astype(post_out.dtype)
            yield

    def mla():
        lat = _mm(h, wm_ref[...])
        ang = freq_ref[...] * pos_ref[...].astype(F32)
        place = place_ref[...]
        cos_hl = jnp.concatenate(_split(jnp.cos(ang)), axis=0)
        sin_hl = jnp.concatenate(_split(jnp.sin(ang)), axis=0)
        yield
        cos = _mm_tn(cos_hl, jnp.concatenate([place[0], place[0]], axis=0)) + base_ref[...]
        sin_up = _mm_tn(sin_hl, jnp.concatenate([place[1], place[1]], axis=0))
        sin_dn = _mm_tn(sin_hl, jnp.concatenate([place[2], place[2]], axis=0))
        seg, invn = seg_ref[...], invn_ref[...]

        def head_norm(t, gain):
            ss = _mm((t * t).astype(BF16), seg[0:t.shape[1], 0:t.shape[1]])
            return t * lax.rsqrt(ss * invn[:, 0:t.shape[1]] + NORM_EPS) * gain

        cq = _rms(lat[:, 0:Q_LORA], gq_ref[...]).astype(BF16)
        ckv = _rms(lat[:, Q_LORA:Q_LORA + KV_LORA], gkv_ref[...]).astype(BF16)
        yield
        kr = _rotary(head_norm(lat[:, Q_LORA + KV_LORA:MLA_W], gkr_ref[...]), cos, sin_up, sin_dn)
        vt_out[0] = (_mm_nt(wvt_ref[...], ckv) + one_ref[...]).astype(BF16)
        yield
        for pair in range(MLA_HEADS // 2):
            q2 = head_norm(_mm(cq, wq_ref[pair]), gqh_ref[...])
            for i in range(2):
                q_out[0, 2 * pair + i] = _rotary(q2[:, i * LANE:(i + 1) * LANE], cos, sin_up, sin_dn).astype(BF16)
            yield
            k2 = head_norm(_mm(ckv, wk_ref[pair]), gkn_ref[...])
            for i in range(2):
                k_out[0, 2 * pair + i] = (k2[:, i * LANE:(i + 1) * LANE] + kr).astype(BF16)
            yield

    _run_in_turn([gates(), rwkv(), mla()])


def _front(x2, pos_row, consts, B, S):
    T = x2.shape[0]
    tm = TM_FRONT
    nj = S // tm
    H = MLA_HEADS
    tok = lambda w: pl.BlockSpec((tm, w), lambda i: (i, 0))
    heads = pl.BlockSpec((1, H, tm, LANE), lambda i: (i // nj, 0, i % nj, 0))
    return pl.pallas_call(
        functools.partial(_front_kernel, tiles_per_seq=nj),
        grid=(T // tm,),
        in_specs=[tok(D_MODEL)] + _halo_specs(tm, D_MODEL, T) + [pl.BlockSpec((1, tm), lambda i: (0, i))]
        + [_const_spec(c.shape) for c in consts],
        out_specs=[tok(2 * D_MODEL), heads, heads, pl.BlockSpec((1, H * V_ROWS, tm), lambda i: (i // nj, 0, i % nj))]
        + [tok(SCAN_W), tok(POST_W)],
        out_shape=[jax.ShapeDtypeStruct((T, 2 * D_MODEL), BF16), jax.ShapeDtypeStruct((B, H, S, LANE), BF16),
                   jax.ShapeDtypeStruct((B, H, S, LANE), BF16), jax.ShapeDtypeStruct((B, H * V_ROWS, S), BF16),
                   jax.ShapeDtypeStruct((T, SCAN_W), F32), jax.ShapeDtypeStruct((T, POST_W), BF16)],
        compiler_params=_params(("parallel",)),
        name="front",
    )(x2, x2, x2, pos_row, *consts)


def _attn_kernel(q_ref, k_ref, vt_ref, ot_ref):
    S = k_ref.shape[2]
    tk = TK_ATTN
    subs = [q_ref[0, 0, i:i + MXU_COLS, :] for i in range(0, q_ref.shape[2], MXU_COLS)]
    nblk = S // tk

    def scores_for(j):
        kb = k_ref[0, 0, j * tk:(j + 1) * tk, :]
        return [_mm_nt(kb, q) for q in subs]

    m = [jnp.full((1, MXU_COLS), -jnp.inf, F32) for _ in subs]
    acc = [jnp.zeros((V_ROWS, MXU_COLS), F32) for _ in subs]
    scores = [scores_for(j) for j in range(min(ATTN_AHEAD, nblk))]
    for j in range(nblk):
        if j + ATTN_AHEAD < nblk:
            scores.append(scores_for(j + ATTN_AHEAD))
        vb = vt_ref[0, :, j * tk:(j + 1) * tk]
        for i, s in enumerate(scores[j]):
            m_new = jnp.maximum(m[i], jnp.max(s, axis=0, keepdims=True))
            p = jnp.exp2(s - m_new)
            alpha = jnp.exp2(m[i] - m_new)
            acc[i] = alpha * acc[i] + _mm(vb, p.astype(BF16))
            m[i] = m_new
        scores[j] = None
    for i in range(len(subs)):
        ot_ref[0, :, i * MXU_COLS:(i + 1) * MXU_COLS] = (acc[i][0:V_HEAD] / acc[i][V_HEAD:V_HEAD + 1]).astype(BF16)


def _attn(q, k, vt):
    B, H, S, _ = q.shape
    tq = TQ_ATTN
    return pl.pallas_call(
        _attn_kernel,
        grid=(B, H, S // tq),
        in_specs=[pl.BlockSpec((1, 1, tq, LANE), lambda b, h, i: (b, h, i, 0)),
                  pl.BlockSpec((1, 1, S, LANE), lambda b, h, i: (b, h, 0, 0)),
                  pl.BlockSpec((1, V_ROWS, S), lambda b, h, i: (b, h, 0))],
        out_specs=pl.BlockSpec((1, V_HEAD, tq), lambda b, h, i: (b, h, i)),
        out_shape=jax.ShapeDtypeStruct((B, H * V_HEAD, S), BF16),
        compiler_params=_params(("parallel", "parallel", "parallel")),
        name="attn",
    )(q, k, vt)


def _wkv_chain(r, k, v, kk, bv, lw, g, reverse):
    C = CHUNK
    sgn = -1 if reverse else 1
    rc = lax.broadcasted_iota(jnp.int32, (C, C), 0)
    cc = lax.broadcasted_iota(jnp.int32, (C, C), 1)
    tri = jnp.where((rc - cc) * sgn >= 0, 1.0, 0.0).astype(BF16)
    lw_hi, lw_lo = _split(lw)
    cum = _mm(tri, lw_hi) + _mm(tri, lw_lo)
    total = jnp.sum(lw, axis=0, keepdims=True)
    e_neg = jnp.exp(-cum)
    e_end = jnp.exp(total - cum)

    row = lax.broadcasted_iota(jnp.int32, (C, GW), 0)
    lane = lax.broadcasted_iota(jnp.int32, (C, GW), 1)
    lane_head = lane // RWKV_HEAD
    ahead = (row - lane % RWKV_HEAD) * sgn
    strict = ahead > 0
    incl = ahead >= 0

    def diag(x):
        xb = x.astype(BF16)
        return jnp.concatenate([jnp.where(lane_head == h, xb, jnp.zeros_like(xb)) for h in range(GROUP)], axis=0)

    def undiag(x):
        return sum(jnp.where(lane_head == h, x[h * C:(h + 1) * C], 0.0) for h in range(GROUP))

    a_t = (-kk * jnp.exp(cum - lw)).astype(BF16)
    r_t = r * jnp.exp(cum)
    dg_v = diag(v)

    yield
    pm = _mm_nt(jnp.concatenate([a_t, r_t.astype(BF16)], axis=0),
                jnp.concatenate([diag(bv * e_neg), diag(k * e_neg)], axis=0))
    n_pow = jnp.where(strict, pm[0:C, 0:GW], 0.0)
    m_ak = jnp.where(strict, pm[0:C, GW:2 * GW], 0.0)
    m_rb = jnp.where(incl, pm[C:2 * C, 0:GW], 0.0).astype(BF16)
    m_rk = jnp.where(incl, pm[C:2 * C, GW:2 * GW], 0.0)
    yield
    wo = _mm(jnp.concatenate([m_ak, m_rk], axis=0).astype(BF16), dg_v)
    w1, o0 = wo[0:C], wo[C:2 * C]

    levels = int(math.log2(C))
    tinv = jnp.where(ahead == 0, 1.0, n_pow)
    for lvl in range(levels):
        yield
        lhs = [] if lvl == levels - 1 else [n_pow]
        lhs += [] if lvl == 0 else [tinv]
        res = _mm(jnp.concatenate(lhs, axis=0).astype(BF16), diag(n_pow))
        if lvl > 0:
            tinv = tinv + res[(len(lhs) - 1) * C:len(lhs) * C]
        n_pow = res[0:C]

    yield
    au = _mm(tinv.astype(BF16), jnp.concatenate([diag(a_t), diag(w1)], axis=1))
    a2, u0 = au[:, 0:GW], au[:, GW:2 * GW]
    yield
    ro = _mm(m_rb, jnp.concatenate([diag(a2), diag(u0)], axis=1))
    r2 = r_t + ro[:, 0:GW]
    o0 = o0 + ro[:, GW:2 * GW]
    yield
    ends = jnp.concatenate([bv * e_end, k * e_end], axis=0).astype(BF16)
    vals = jnp.concatenate([jnp.concatenate([a2, u0], axis=1),
                            jnp.concatenate([jnp.zeros_like(v), v], axis=1)], axis=0).astype(BF16)
    pp = _mm_tn(ends, vals)
    phi, psi = undiag(pp[:, 0:GW]), undiag(pp[:, GW:2 * GW])
    yield
    res = _mm(jnp.concatenate([r2, phi], axis=0).astype(BF16), diag(g))
    on_diag = jnp.where(ahead == 0, jnp.exp(total), 0.0)
    p_end = sum(jnp.where(lane_head == h, jnp.sum(jnp.where(lane_head == h, on_diag, 0.0), axis=1, keepdims=True), 0.0)
                for h in range(GROUP))
    yield res[0:C] + o0, p_end * g + res[C:2 * C] + psi


def _wkv_kernel(fwd_ref, bwd_ref, of_ref, ob_ref, g_ref):
    @pl.when(pl.program_id(1) == 0)
    def _():
        g_ref[...] = jnp.zeros_like(g_ref)

    chains = []
    for bi in range(of_ref.shape[0]):
        for d, (in_ref, o_ref) in enumerate(((fwd_ref, of_ref), (bwd_ref, ob_ref))):
            for gidx in range(RWKV_WIDTH // GW):
                sl = slice(gidx * GW, (gidx + 1) * GW)
                ins = [in_ref[bi, :, seg.start + sl.start:seg.start + sl.stop]
                       for seg in (SCAN_R, SCAN_K, SCAN_V, SCAN_KK, SCAN_BV, SCAN_LW[d])]
                gen = _wkv_chain(*ins, g_ref[bi, d, gidx], reverse=d == 1)
                chains.append((gen, o_ref, bi, sl, d, gidx))
    while chains:
        for chain in list(chains):
            gen, o_ref, bi, sl, d, gidx = chain
            out = next(gen)
            if out is not None:
                o_ref[bi, :, sl] = out[0].astype(o_ref.dtype)
                g_ref[bi, d, gidx] = out[1]
                chains.remove(chain)


def _wkv(scan, B, S):
    T = scan.shape[0]
    W = RWKV_WIDTH
    C = CHUNK
    nb = NB_WKV
    nc = S // C
    blk = lambda w, rev: pl.BlockSpec((nb, C, w), lambda b, c: (b, nc - 1 - c if rev else c, 0))
    out = jax.ShapeDtypeStruct((B, S, W), BF16)
    scan = scan.reshape(B, S, SCAN_W)
    o_f, o_b = pl.pallas_call(
        _wkv_kernel,
        grid=(B // nb, nc),
        in_specs=[blk(SCAN_W, False), blk(SCAN_W, True)],
        out_specs=[blk(W, False), blk(W, True)],
        out_shape=[out, out],
        scratch_shapes=[pltpu.VMEM((nb, 2, W // GW, RWKV_HEAD, GW), F32)],
        compiler_params=_params(("parallel", "arbitrary")),
        name="wkv",
    )(scan, scan)
    return o_f.reshape(T, W), o_b.reshape(T, W)


def _mix_kernel(of_ref, ob_ref, post_ref, oat_ref, gate_ref, x_ref, bo_ref, lng_ref, lnb_ref,
                woa_ref, wob_ref, wm_ref, out_ref):
    bo = bo_ref[...]
    bonus = post_ref[:, POST_BONUS].astype(F32)
    ob = None
    for d, o_ref in enumerate((of_ref, ob_ref)):
        o = o_ref[...].astype(F32)
        mu = _head_sums(o, bo) / RWKV_HEAD
        dl = o - mu
        var = _head_sums(dl * dl, bo) / RWKV_HEAD
        y = dl * lax.rsqrt(var + GN_EPS) * lng_ref[...] + lnb_ref[...] + bonus
        y = y * post_ref[:, POST_GATE[d]]
        ob = y if ob is None else ob + y
    y_b = _mm(ob.astype(BF16), wob_ref[...])
    y_a = _mm_tn(oat_ref[0], woa_ref[...])
    mixed = gate_ref[:, 0:D_MODEL] * y_a + gate_ref[:, D_MODEL:2 * D_MODEL] * y_b
    out_ref[...] = x_ref[...] + _mm(mixed.astype(BF16), wm_ref[...])


def _mix(o_f, o_b, post, oat, gates, x2, bo, lng, lnb, woa, wob, wm, B, S):
    T = x2.shape[0]
    tm = TM_MIX
    nj = S // tm
    consts = [bo, lng, lnb, woa, wob, wm]
    tok = lambda w: pl.BlockSpec((tm, w), lambda b, j: (b * nj + j, 0))
    return pl.pallas_call(
        _mix_kernel,
        grid=(B, nj),
        in_specs=[tok(RWKV_WIDTH), tok(RWKV_WIDTH), tok(POST_W),
                  pl.BlockSpec((1, MLA_HEADS * V_HEAD, tm), lambda b, j: (b, 0, j)),
                  pl.BlockSpec((tm, 2 * D_MODEL), lambda b, j: (b * nj + j, 0)),
                  pl.BlockSpec((tm, D_MODEL), lambda b, j: (b * nj + j, 0))]
        + [_const_spec(c.shape) for c in consts],
        out_specs=pl.BlockSpec((tm, D_MODEL), lambda b, j: (b * nj + j, 0)),
        out_shape=jax.ShapeDtypeStruct((T, D_MODEL), F32),
        compiler_params=_params(("parallel", "parallel")),
        name="mix",
    )(o_f, o_b, post, oat, gates, x2, *consts)


def _ffn_kernel(x_ref, before_ref, after_ref, g_ref, wg_ref, wu_ref, cw_ref, cb_ref, wd_ref, out_ref, *, tiles_per_seq):
    tm = x_ref.shape[0]
    x = x_ref[...]
    h_all = _rms(_with_neighbours(x, before_ref, after_ref, tiles_per_seq), g_ref[...]).astype(BF16)
    h = h_all[0:tm]
    chunks = [slice(c, c + TH_FFN) for c in range(0, FFN_HIDDEN, TH_FFN)]
    pres = [_mm(h_all, wg_ref[:, sl]) for sl in chunks]
    ups = [_mm(h, wu_ref[:, sl]) for sl in chunks]
    acc = x
    for sl, pre_all, up in zip(chunks, pres, ups):
        pre = pre_all[0:tm]
        prev, nxt = _shifted(pre, pre_all[tm:tm + 1, :], pre_all[tm + SUBLANE:tm + SUBLANE + 1, :])
        gp = prev * cw_ref[0:1, sl] + pre * cw_ref[1:2, sl] + nxt * cw_ref[2:3, sl] + cb_ref[:, sl]
        act = gp * _sigmoid(gp) * up
        acc = acc + _mm(act.astype(BF16), wd_ref[sl, :])
    out_ref[...] = acc


def _ffn(x1, g, wg, wu, cw, cb, wd, S):
    T = x1.shape[0]
    tm = TM_FFN
    consts = [g, wg, wu, cw, cb, wd]
    return pl.pallas_call(
        functools.partial(_ffn_kernel, tiles_per_seq=S // tm),
        grid=(T // tm,),
        in_specs=[pl.BlockSpec((tm, D_MODEL), lambda i: (i, 0))] + _halo_specs(tm, D_MODEL, T)
        + [_const_spec(c.shape) for c in consts],
        out_specs=pl.BlockSpec((tm, D_MODEL), lambda i: (i, 0)),
        out_shape=jax.ShapeDtypeStruct((T, D_MODEL), F32),
        compiler_params=_params(("parallel",)),
        name="ffn",
    )(x1, x1, x1, *consts)


def _pad_cols(w, width):
    return jnp.pad(w, ((0, 0), (0, width - w.shape[1])))


def _pad_rows(w, height):
    return jnp.pad(w, ((0, height - w.shape[0]), (0, 0)))


def _rw_layout(t):
    W = RWKV_WIDTH
    pad = lambda a: jnp.pad(a, [(0, 0)] * (a.ndim - 1) + [(0, LANE - a.shape[-1])])
    o = 3 * W
    return jnp.concatenate([t[..., 0:o], pad(t[..., o:o + 64]), pad(t[..., o + 64:o + 128]),
                            pad(t[..., o + 128:o + 192]), t[..., o + 192:o + 448]], axis=-1)


def kernel(x, positions, norm_mix_g, w_in, b_gate, q_a_norm_g, kv_a_norm_g, w_uq, w_ukv, qn_norm_g, qr_norm_g,
           kn_norm_g, kr_norm_g, shift_mu, w0, w2, a0, a2, g2, k_k, k_a, r_k, ln_x_g, ln_x_b, w_o, w_merge,
           norm_ffn_g, w_ffn_gate, w_ffn_up, ffn_conv_w, ffn_conv_b, w_ffn_down):
    B, S, D = x.shape
    T = B * S
    depth = norm_mix_g.shape[0]
    H = MLA_HEADS
    mla_cols = Q_LORA + KV_LORA + QK_ROPE
    rw_cols = 3 * RWKV_WIDTH + A_LORA + 2 * DECAY_LORA + 2 * GATE_LORA

    half = QK_ROPE // 2
    inv_freq = ROPE_THETA ** (-jnp.arange(0, QK_ROPE, 2, dtype=F32) / QK_ROPE)
    freq = inv_freq.reshape(half, 1)
    fr = jnp.arange(half)
    place = jnp.zeros((3, half, LANE), F32)
    place = place.at[0, fr, QK_NOPE + fr].set(1.0).at[0, fr, QK_NOPE + half + fr].set(1.0)
    place = place.at[1, fr, QK_NOPE + fr].set(-1.0)
    place = place.at[2, fr, QK_NOPE + half + fr].set(1.0)
    place = place.astype(BF16)
    lane = jnp.arange(LANE)
    base = ((lane < QK_NOPE) | (lane >= QK_NOPE + QK_ROPE)).astype(F32).reshape(1, LANE)
    lane2 = jnp.arange(2 * LANE)
    seg_of = 2 * (lane2 // LANE) + (lane2 % LANE >= QK_NOPE)
    seg = (seg_of[:, None] == seg_of[None, :]).astype(BF16)
    invn = jnp.where(lane2 % LANE < QK_NOPE, 1.0 / QK_NOPE, 1.0 / QK_ROPE).astype(F32).reshape(1, 2 * LANE)
    head_of = jnp.arange(GW) // RWKV_HEAD
    bo = (head_of[:, None] == head_of[None, :]).astype(BF16)
    pos_row = positions.reshape(1, T)
    row = lambda v: v.reshape(1, -1).astype(F32)
    scale = math.log2(math.e) / math.sqrt(QK_NOPE + QK_ROPE)

    x2 = x.reshape(T, D)
    for l in range(depth):
        wi = w_in[l]
        w_mla = jnp.concatenate([wi[:, 0:Q_LORA + KV_LORA], jnp.zeros((D, QK_NOPE), F32),
                                 wi[:, Q_LORA + KV_LORA:mla_cols], jnp.zeros((D, LANE - QK_NOPE - QK_ROPE), F32)], axis=1)
        w_rw = _rw_layout(wi[:, mla_cols:mla_cols + rw_cols])
        w_gate = wi[:, mla_cols + rw_cols:]
        wq = _pad_cols(w_uq[l].reshape(Q_LORA, H, QK_NOPE + QK_ROPE).transpose(1, 0, 2).reshape(H * Q_LORA, -1), LANE)
        wq = wq.reshape(H // 2, 2, Q_LORA, LANE).transpose(0, 2, 1, 3).reshape(H // 2, Q_LORA, 2 * LANE).astype(BF16)
        wkv = w_ukv[l].reshape(KV_LORA, H, QK_NOPE + V_HEAD)
        wk = _pad_cols(wkv[:, :, :QK_NOPE].transpose(1, 0, 2).reshape(H * KV_LORA, QK_NOPE), LANE)
        wk = wk.reshape(H // 2, 2, KV_LORA, LANE).transpose(0, 2, 1, 3).reshape(H // 2, KV_LORA, 2 * LANE).astype(BF16)
        wvt = jnp.pad(wkv[:, :, QK_NOPE:], ((0, 0), (0, 0), (0, V_ROWS - V_HEAD))).reshape(KV_LORA, H * V_ROWS).T.astype(BF16)
        one = jnp.tile((jnp.arange(V_ROWS) == V_HEAD).astype(F32), H).reshape(H * V_ROWS, 1)
        gqh = jnp.tile(_pad_cols(jnp.concatenate([qn_norm_g[l], qr_norm_g[l]]).reshape(1, -1), LANE) * scale, (1, 2))
        gkn = jnp.tile(_pad_cols(kn_norm_g[l].reshape(1, -1), LANE), (1, 2))
        gkr = _pad_cols(jnp.concatenate([jnp.zeros((QK_NOPE,), F32), kr_norm_g[l]]).reshape(1, -1), LANE)
        mu = _rw_layout(shift_mu[l])
        mu = jnp.concatenate([mu, 1.0 - mu[0:1] - mu[1:2]], axis=0)
        a2p = _pad_rows(a2[l], LANE).astype(BF16)
        w2p = jnp.pad(w2[l], ((0, 0), (0, LANE - DECAY_LORA), (0, 0))).astype(BF16)
        consts = [row(norm_mix_g[l]), w_mla.astype(BF16), w_rw.astype(BF16), w_gate.astype(BF16),
                  b_gate[l].reshape(1, 2 * D),
                  row(q_a_norm_g[l]), row(kv_a_norm_g[l]), wq, wk, wvt, gqh, gkn, gkr, freq, place, base, one, seg, invn,
                  mu, row(a0[l]), a2p, w0[l], w2p, g2[l].astype(BF16), row(k_k[l]), row(k_a[l]), row(r_k[l]), bo]

        gates, q, k, vt, scan, post = _front(x2, pos_row, consts, B, S)
        oat = _attn(q, k, vt)
        o_f, o_b = _wkv(scan, B, S)
        wo = w_o[l].astype(BF16)
        x2 = _mix(o_f, o_b, post, oat, gates, x2, bo, row(ln_x_g[l]), row(ln_x_b[l]), wo[:H * V_HEAD],
                  wo[H * V_HEAD:], w_merge[l].astype(BF16), B, S)
        x2 = _ffn(x2, row(norm_ffn_g[l]), w_ffn_gate[l].astype(BF16), w_ffn_up[l].astype(BF16), ffn_conv_w[l],
                  ffn_conv_b[l].reshape(1, -1), w_ffn_down[l].astype(BF16), S)
    return x2.reshape(B, S, D)
```

```python
import functools
import math

import jax
import jax.numpy as jnp
from jax import lax
from jax.experimental import pallas as pl
from jax.experimental.pallas import tpu as pltpu

F32 = jnp.float32
BF16 = jnp.bfloat16

D_MODEL = 1024
MLA_HEADS = 8
QK_NOPE = 64
QK_ROPE = 32
V_HEAD = 64
Q_LORA = 256
KV_LORA = 128
ROPE_THETA = 10000.0
RWKV_HEADS = 8
RWKV_HEAD = 64
RWKV_WIDTH = RWKV_HEADS * RWKV_HEAD
DECAY_LORA = 64
A_LORA = 64
GATE_LORA = 128
GN_EPS = 64e-5
FFN_HIDDEN = 2816
NORM_EPS = 1e-6

LANE = 128
SUBLANE = 8
MXU_COLS = 256
VMEM_LIMIT = 56 * 1024 * 1024

V_ROWS = V_HEAD + 16
MLA_W = 512
RW_W = 3 * RWKV_WIDTH + 5 * LANE

CHUNK = 64
GROUP = 4
GW = GROUP * RWKV_HEAD
assert CHUNK == RWKV_HEAD

_seg = lambda i: slice(i * RWKV_WIDTH, (i + 1) * RWKV_WIDTH)
SCAN_R, SCAN_K, SCAN_V, SCAN_KK, SCAN_BV = (_seg(i) for i in range(5))
SCAN_LW = (_seg(5), _seg(6))
SCAN_W = 7 * RWKV_WIDTH
POST_BONUS = _seg(0)
POST_GATE = (_seg(1), _seg(2))
POST_W = 3 * RWKV_WIDTH

TM_FRONT = 256
GATE_CHUNK = MXU_COLS
TQ_ATTN = 2048
TK_ATTN = 256
ATTN_AHEAD = 1
NB_WKV = 8
TM_MIX = 512
TM_FFN = 512
TH_FFN = 1408


def _mm(a, b):
    return jnp.dot(a, b, preferred_element_type=F32)


def _mm_nt(a, b):
    return lax.dot_general(a, b, (((1,), (1,)), ((), ())), preferred_element_type=F32)


def _mm_tn(a, b):
    return lax.dot_general(a, b, (((0,), (0,)), ((), ())), preferred_element_type=F32)


def _split(x):
    hi = x.astype(BF16)
    lo = (x - hi.astype(F32)).astype(BF16)
    return hi, lo


def _head_sums(x, bo):
    xb = x.astype(BF16)
    return jnp.concatenate([_mm(xb[:, c:c + GW], bo) for c in range(0, x.shape[1], GW)], axis=1)


def _sigmoid(x):
    return 0.5 * jnp.tanh(0.5 * x) + 0.5


def _rms(x, g, eps=NORM_EPS):
    return x * lax.rsqrt(jnp.mean(x * x, axis=-1, keepdims=True) + eps) * g


def _const_spec(shape):
    nd = len(shape)
    return pl.BlockSpec(shape, lambda *_: (0,) * nd, pipeline_mode=pl.Buffered(1))


def _params(sem):
    return pltpu.CompilerParams(dimension_semantics=sem, vmem_limit_bytes=VMEM_LIMIT)


def _run_in_turn(streams):
    streams = list(streams)
    while streams:
        for st in list(streams):
            if next(st, "done") == "done":
                streams.remove(st)


def _shifted(p, before, after):
    tm = p.shape[0]
    row = lax.broadcasted_iota(jnp.int32, p.shape, 0)
    prev = jnp.where(row == 0, before, pltpu.roll(p, 1, axis=0))
    nxt = jnp.where(row == tm - 1, after, pltpu.roll(p, tm - 1, axis=0))
    return prev, nxt


def _halo_specs(tm, width, T):
    per = tm // SUBLANE
    before = pl.BlockSpec((SUBLANE, width), lambda i: (jnp.maximum(i * per - 1, 0), 0))
    after = pl.BlockSpec((SUBLANE, width), lambda i: (jnp.minimum((i + 1) * per, T // SUBLANE - 1), 0))
    return [before, after]


def _with_neighbours(x, before_ref, after_ref, tiles_per_seq):
    j = pl.program_id(0) % tiles_per_seq
    before = jnp.where(j == 0, 0.0, before_ref[SUBLANE - 1:SUBLANE, :])
    after = jnp.where(j == tiles_per_seq - 1, 0.0, after_ref[0:1, :])
    width = x.shape[1]
    return jnp.concatenate([x, jnp.broadcast_to(before, (SUBLANE, width)), jnp.broadcast_to(after, (SUBLANE, width))],
                           axis=0)


def _rotary(t, cos, sin_up, sin_dn):
    up = pltpu.roll(t, LANE - QK_ROPE // 2, axis=1)
    dn = pltpu.roll(t, QK_ROPE // 2, axis=1)
    return t * cos + up * sin_up + dn * sin_dn


def _front_kernel(x_ref, xb_ref, xa_ref, pos_ref, gmix_ref, wm_ref, wr_ref, wg_ref, bg_ref,
                  gq_ref, gkv_ref, wq_ref, wk_ref, wvt_ref, gqh_ref, gkn_ref, gkr_ref, freq_ref, place_ref, base_ref,
                  one_ref, seg_ref, invn_ref,
                  mu_ref, a0_ref, a2_ref, w0_ref, w2_ref, g2_ref, kk_ref, ka_ref, rk_ref, bo_ref,
                  gate_out, q_out, k_out, vt_out,
                  scan_out, post_out, *, tiles_per_seq):
    tm = x_ref.shape[0]
    gmix = gmix_ref[...]
    h_all = _rms(_with_neighbours(x_ref[...], xb_ref, xa_ref, tiles_per_seq), gmix).astype(BF16)
    h = h_all[0:tm]

    def gates():
        for c in range(0, 2 * D_MODEL, GATE_CHUNK):
            sl = slice(c, c + GATE_CHUNK)
            gate_out[:, sl] = _sigmoid(_mm(h, wg_ref[:, sl]) + bg_ref[:, sl]).astype(gate_out.dtype)
            yield

    def rwkv():
        p_all = _mm(h_all, wr_ref[...])
        yield
        p = p_all[0:tm]
        prev, nxt = _shifted(p, p_all[tm:tm + 1, :], p_all[tm + SUBLANE:tm + SUBLANE + 1, :])
        u = p * mu_ref[2:3, :] + prev * mu_ref[0:1, :] + nxt * mu_ref[1:2, :]
        W = RWKV_WIDTH
        r, k, v = u[:, 0:W], u[:, W:2 * W], u[:, 2 * W:3 * W]
        scan_out[:, SCAN_R] = r
        scan_out[:, SCAN_V] = v
        yield
        a = _sigmoid(a0_ref[...] + _mm(u[:, 3 * W:3 * W + LANE].astype(BF16), a2_ref[...]))
        bo = bo_ref[...]
        kkf = k * kk_ref[...]
        ss = _head_sums(kkf * kkf, bo)
        yield
        kkn = kkf * jnp.where(ss >= 1e-24, lax.rsqrt(ss), 1e12)
        k = k * (1.0 + (a - 1.0) * ka_ref[...])
        scan_out[:, SCAN_K] = k
        scan_out[:, SCAN_KK] = kkn
        scan_out[:, SCAN_BV] = kkn * a
        yield
        post_out[:, POST_BONUS] = (_head_sums(r * k * rk_ref[...], bo) * v).astype(post_out.dtype)
        yield
        for d in range(2):
            dlo = u[:, 3 * W + (1 + d) * LANE:3 * W + (2 + d) * LANE]
            glo = u[:, 3 * W + (3 + d) * LANE:3 * W + (4 + d) * LANE]
            z = -(w0_ref[d:d + 1, :] + _mm(jnp.tanh(dlo).astype(BF16), w2_ref[d]))
            softplus = jnp.maximum(z, 0.0) + jnp.log(1.0 + jnp.exp(-jnp.abs(z)))
            scan_out[:, SCAN_LW[d]] = -jnp.exp(-softplus - 0.5)
            yield
            post_out[:, POST_GATE[d]] = _mm(_sigmoid(glo).astype(BF16), g2_ref[d]).astype(post_out.dtype)
            yield

    def mla():
        lat = _mm(h, wm_ref[...])
        ang = freq_ref[...] * pos_ref[...].astype(F32)
        place = place_ref[...]
        cos_hl = jnp.concatenate(_split(jnp.cos(ang)), axis=0)
        sin_hl = jnp.concatenate(_split(jnp.sin(ang)), axis=0)
        yield
        cos = _mm_tn(cos_hl, jnp.concatenate([place[0], place[0]], axis=0)) + base_ref[...]
        sin_up = _mm_tn(sin_hl, jnp.concatenate([place[1], place[1]], axis=0))
        sin_dn = _mm_tn(sin_hl, jnp.concatenate([place[2], place[2]], axis=0))
        seg, invn = seg_ref[...], invn_ref[...]

        def head_norm(t, gain):
            ss = _mm((t * t).astype(BF16), seg[0:t.shape[1], 0:t.shape[1]])
            return t * lax.rsqrt(ss * invn[:, 0:t.shape[1]] + NORM_EPS) * gain

        cq = _rms(lat[:, 0:Q_LORA], gq_ref[...]).astype(BF16)
        ckv = _rms(lat[:, Q_LORA:Q_LORA + KV_LORA], gkv_ref[...]).astype(BF16)
        yield
        kr = _rotary(head_norm(lat[:, Q_LORA + KV_LORA:MLA_W], gkr_ref[...]), cos, sin_up, sin_dn)
        vt_out[0] = (_mm_nt(wvt_ref[...], ckv) + one_ref[...]).astype(BF16)
        yield
        for pair in range(MLA_HEADS // 2):
            q2 = head_norm(_mm(cq, wq_ref[pair]), gqh_ref[...])
            for i in range(2):
                q_out[0, 2 * pair + i] = _rotary(q2[:, i * LANE:(i + 1) * LANE], cos, sin_up, sin_dn).astype(BF16)
            yield
            k2 = head_norm(_mm(ckv, wk_ref[pair]), gkn_ref[...])
            for i in range(2):
                k_out[0, 2 * pair + i] = (k2[:, i * LANE:(i + 1) * LANE] + kr).astype(BF16)
            yield

    _run_in_turn([gates(), rwkv(), mla()])


def _front(x2, pos_row, consts, B, S):
    T = x2.shape[0]
    tm = TM_FRONT
    nj = S // tm
    H = MLA_HEADS
    tok = lambda w: pl.BlockSpec((tm, w), lambda i: (i, 0))
    heads = pl.BlockSpec((1, H, tm, LANE), lambda i: (i // nj, 0, i % nj, 0))
    return pl.pallas_call(
        functools.partial(_front_kernel, tiles_per_seq=nj),
        grid=(T // tm,),
        in_specs=[tok(D_MODEL)] + _halo_specs(tm, D_MODEL, T) + [pl.BlockSpec((1, tm), lambda i: (0, i))]
        + [_const_spec(c.shape) for c in consts],
        out_specs=[tok(2 * D_MODEL), heads, heads, pl.BlockSpec((1, H * V_ROWS, tm), lambda i: (i // nj, 0, i % nj))]
        + [tok(SCAN_W), tok(POST_W)],
        out_shape=[jax.ShapeDtypeStruct((T, 2 * D_MODEL), BF16), jax.ShapeDtypeStruct((B, H, S, LANE), BF16),
                   jax.ShapeDtypeStruct((B, H, S, LANE), BF16), jax.ShapeDtypeStruct((B, H * V_ROWS, S), BF16),
                   jax.ShapeDtypeStruct((T, SCAN_W), F32), jax.ShapeDtypeStruct((T, POST_W), BF16)],
        compiler_params=_params(("parallel",)),
        name="front",
    )(x2, x2, x2, pos_row, *consts)


def _attn_kernel(q_ref, k_ref, vt_ref, ot_ref):
    S = k_ref.shape[2]
    tk = TK_ATTN
    subs = [q_ref[0, 0, i:i + MXU_COLS, :] for i in range(0, q_ref.shape[2], MXU_COLS)]
    nblk = S // tk

    def scores_for(j):
        kb = k_ref[0, 0, j * tk:(j + 1) * tk, :]
        return [_mm_nt(kb, q) for q in subs]

    m = [jnp.full((1, MXU_COLS), -jnp.inf, F32) for _ in subs]
    acc = [jnp.zeros((V_ROWS, MXU_COLS), F32) for _ in subs]
    scores = [scores_for(j) for j in range(min(ATTN_AHEAD, nblk))]
    for j in range(nblk):
        if j + ATTN_AHEAD < nblk:
            scores.append(scores_for(j + ATTN_AHEAD))
        vb = vt_ref[0, :, j * tk:(j + 1) * tk]
        for i, s in enumerate(scores[j]):
            m_new = jnp.maximum(m[i], jnp.max(s, axis=0, keepdims=True))
            p = jnp.exp2(s - m_new)
            alpha = jnp.exp2(m[i] - m_new)
            acc[i] = alpha * acc[i] + _mm(vb, p.astype(BF16))
            m[i] = m_new
        scores[j] = None
    for i in range(len(subs)):
        ot_ref[0, :, i * MXU_COLS:(i + 1) * MXU_COLS] = (acc[i][0:V_HEAD] / acc[i][V_HEAD:V_HEAD + 1]).astype(BF16)


def _attn(q, k, vt):
    B, H, S, _ = q.shape
    tq = TQ_ATTN
    return pl.pallas_call(
        _attn_kernel,
        grid=(B, H, S // tq),
        in_specs=[pl.BlockSpec((1, 1, tq, LANE), lambda b, h, i: (b, h, i, 0)),
                  pl.BlockSpec((1, 1, S, LANE), lambda b, h, i: (b, h, 0, 0)),
                  pl.BlockSpec((1, V_ROWS, S), lambda b, h, i: (b, h, 0))],
        out_specs=pl.BlockSpec((1, V_HEAD, tq), lambda b, h, i: (b, h, i)),
        out_shape=jax.ShapeDtypeStruct((B, H * V_HEAD, S), BF16),
        compiler_params=_params(("parallel", "parallel", "parallel")),
        name="attn",
    )(q, k, vt)


def _wkv_chain(r, k, v, kk, bv, lw, g, reverse):
    C = CHUNK
    sgn = -1 if reverse else 1
    rc = lax.broadcasted_iota(jnp.int32, (C, C), 0)
    cc = lax.broadcasted_iota(jnp.int32, (C, C), 1)
    tri = jnp.where((rc - cc) * sgn >= 0, 1.0, 0.0).astype(BF16)
    lw_hi, lw_lo = _split(lw)
    cum = _mm(tri, lw_hi) + _mm(tri, lw_lo)
    total = jnp.sum(lw, axis=0, keepdims=True)
    e_neg = jnp.exp(-cum)
    e_end = jnp.exp(total - cum)

    row = lax.broadcasted_iota(jnp.int32, (C, GW), 0)
    lane = lax.broadcasted_iota(jnp.int32, (C, GW), 1)
    lane_head = lane // RWKV_HEAD
    ahead = (row - lane % RWKV_HEAD) * sgn
    strict = ahead > 0
    incl = ahead >= 0

    def diag(x):
        xb = x.astype(BF16)
        return jnp.concatenate([jnp.where(lane_head == h, xb, jnp.zeros_like(xb)) for h in range(GROUP)], axis=0)

    def undiag(x):
        return sum(jnp.where(lane_head == h, x[h * C:(h + 1) * C], 0.0) for h in range(GROUP))

    a_t = (-kk * jnp.exp(cum - lw)).astype(BF16)
    r_t = r * jnp.exp(cum)
    dg_v = diag(v)

    yield
    pm = _mm_nt(jnp.concatenate([a_t, r_t.astype(BF16)], axis=0),
                jnp.concatenate([diag(bv * e_neg), diag(k * e_neg)], axis=0))
    n_pow = jnp.where(strict, pm[0:C, 0:GW], 0.0)
    m_ak = jnp.where(strict, pm[0:C, GW:2 * GW], 0.0)
    m_rb = jnp.where(incl, pm[C:2 * C, 0:GW], 0.0).astype(BF16)
    m_rk = jnp.where(incl, pm[C:2 * C, GW:2 * GW], 0.0)
    yield
    wo = _mm(jnp.concatenate([m_ak, m_rk], axis=0).astype(BF16), dg_v)
    w1, o0 = wo[0:C], wo[C:2 * C]

    levels = int(math.log2(C))
    tinv = jnp.where(ahead == 0, 1.0, n_pow)
    for lvl in range(levels):
        yield
        lhs = [] if lvl == levels - 1 else [n_pow]
        lhs += [] if lvl == 0 else [tinv]
        res = _mm(jnp.concatenate(lhs, axis=0).astype(BF16), diag(n_pow))
        if lvl > 0:
            tinv = tinv + res[(len(lhs) - 1) * C:len(lhs) * C]
        n_pow = res[0:C]

    yield
    au = _mm(tinv.astype(BF16), jnp.concatenate([diag(a_t), diag(w1)], axis=1))
    a2, u0 = au[:, 0:GW], au[:, GW:2 * GW]
    yield
    ro = _mm(m_rb, jnp.concatenate([diag(a2), diag(u0)], axis=1))
    r2 = r_t + ro[:, 0:GW]
    o0 = o0 + ro[:, GW:2 * GW]
    yield
    ends = jnp.concatenate([bv * e_end, k * e_end], axis=0).astype(BF16)
    vals = jnp.concatenate([jnp.concatenate([a2, u0], axis=1),
                            jnp.concatenate([jnp.zeros_like(v), v], axis=1)], axis=0).astype(BF16)
    pp = _mm_tn(ends, vals)
    phi, psi = undiag(pp[:, 0:GW]), undiag(pp[:, GW:2 * GW])
    yield
    res = _mm(jnp.concatenate([r2, phi], axis=0).astype(BF16), diag(g))
    on_diag = jnp.where(ahead == 0, jnp.exp(total), 0.0)
    p_end = sum(jnp.where(lane_head == h, jnp.sum(jnp.where(lane_head == h, on_diag, 0.0), axis=1, keepdims=True), 0.0)
                for h in range(GROUP))
    yield res[0:C] + o0, p_end * g + res[C:2 * C] + psi


def _wkv_kernel(fwd_ref, bwd_ref, of_ref, ob_ref, g_ref):
    @pl.when(pl.program_id(1) == 0)
    def _():
        g_ref[...] = jnp.zeros_like(g_ref)

    chains = []
    for bi in range(of_ref.shape[0]):
        for d, (in_ref, o_ref) in enumerate(((fwd_ref, of_ref), (bwd_ref, ob_ref))):
            for gidx in range(RWKV_WIDTH // GW):
                sl = slice(gidx * GW, (gidx + 1) * GW)
                ins = [in_ref[bi, :, seg.start + sl.start:seg.start + sl.stop]
                       for seg in (SCAN_R, SCAN_K, SCAN_V, SCAN_KK, SCAN_BV, SCAN_LW[d])]
                gen = _wkv_chain(*ins, g_ref[bi, d, gidx], reverse=d == 1)
                chains.append((gen, o_ref, bi, sl, d, gidx))
    while chains:
        for chain in list(chains):
            gen, o_ref, bi, sl, d, gidx = chain
            out = next(gen)
            if out is not None:
                o_ref[bi, :, sl] = out[0].astype(o_ref.dtype)
                g_ref[bi, d, gidx] = out[1]
                chains.remove(chain)


def _wkv(scan, B, S):
    T = scan.shape[0]
    W = RWKV_WIDTH
    C = CHUNK
    nb = NB_WKV
    nc = S // C
    blk = lambda w, rev: pl.BlockSpec((nb, C, w), lambda b, c: (b, nc - 1 - c if rev else c, 0))
    out = jax.ShapeDtypeStruct((B, S, W), BF16)
    scan = scan.reshape(B, S, SCAN_W)
    o_f, o_b = pl.pallas_call(
        _wkv_kernel,
        grid=(B // nb, nc),
        in_specs=[blk(SCAN_W, False), blk(SCAN_W, True)],
        out_specs=[blk(W, False), blk(W, True)],
        out_shape=[out, out],
        scratch_shapes=[pltpu.VMEM((nb, 2, W // GW, RWKV_HEAD, GW), F32)],
        compiler_params=_params(("parallel", "arbitrary")),
        name="wkv",
    )(scan, scan)
    return o_f.reshape(T, W), o_b.reshape(T, W)


def _mix_kernel(of_ref, ob_ref, post_ref, oat_ref, gate_ref, x_ref, bo_ref, lng_ref, lnb_ref,
                woa_ref, wob_ref, wm_ref, out_ref):
    bo = bo_ref[...]
    bonus = post_ref[:, POST_BONUS].astype(F32)
    ob = None
    for d, o_ref in enumerate((of_ref, ob_ref)):
        o = o_ref[...].astype(F32)
        mu = _head_sums(o, bo) / RWKV_HEAD
        dl = o - mu
        var = _head_sums(dl * dl, bo) / RWKV_HEAD
        y = dl * lax.rsqrt(var + GN_EPS) * lng_ref[...] + lnb_ref[...] + bonus
        y = y * post_ref[:, POST_GATE[d]]
        ob = y if ob is None else ob + y
    y_b = _mm(ob.astype(BF16), wob_ref[...])
    y_a = _mm_tn(oat_ref[0], woa_ref[...])
    mixed = gate_ref[:, 0:D_MODEL] * y_a + gate_ref[:, D_MODEL:2 * D_MODEL] * y_b
    out_ref[...] = x_ref[...] + _mm(mixed.astype(BF16), wm_ref[...])


def _mix(o_f, o_b, post, oat, gates, x2, bo, lng, lnb, woa, wob, wm, B, S):
    T = x2.shape[0]
    tm = TM_MIX
    nj = S // tm
    consts = [bo, lng, lnb, woa, wob, wm]
    tok = lambda w: pl.BlockSpec((tm, w), lambda b, j: (b * nj + j, 0))
    return pl.pallas_call(
        _mix_kernel,
        grid=(B, nj),
        in_specs=[tok(RWKV_WIDTH), tok(RWKV_WIDTH), tok(POST_W),
                  pl.BlockSpec((1, MLA_HEADS * V_HEAD, tm), lambda b, j: (b, 0, j)),
                  pl.BlockSpec((tm, 2 * D_MODEL), lambda b, j: (b * nj + j, 0)),
                  pl.BlockSpec((tm, D_MODEL), lambda b, j: (b * nj + j, 0))]
        + [_const_spec(c.shape) for c in consts],
        out_specs=pl.BlockSpec((tm, D_MODEL), lambda b, j: (b * nj + j, 0)),
        out_shape=jax.ShapeDtypeStruct((T, D_MODEL), F32),
        compiler_params=_params(("parallel", "parallel")),
        name="mix",
    )(o_f, o_b, post, oat, gates, x2, *consts)


def _ffn_kernel(x_ref, before_ref, after_ref, g_ref, wg_ref, wu_ref, cw_ref, cb_ref, wd_ref, out_ref, *, tiles_per_seq):
    tm = x_ref.shape[0]
    x = x_ref[...]
    h_all = _rms(_with_neighbours(x, before_ref, after_ref, tiles_per_seq), g_ref[...]).astype(BF16)
    h = h_all[0:tm]
    chunks = [slice(c, c + TH_FFN) for c in range(0, FFN_HIDDEN, TH_FFN)]
    pres = [_mm(h_all, wg_ref[:, sl]) for sl in chunks]
    ups = [_mm(h, wu_ref[:, sl]) for sl in chunks]
    acc = x
    for sl, pre_all, up in zip(chunks, pres, ups):
        pre = pre_all[0:tm]
        prev, nxt = _shifted(pre, pre_all[tm:tm + 1, :], pre_all[tm + SUBLANE:tm + SUBLANE + 1, :])
        gp = prev * cw_ref[0:1, sl] + pre * cw_ref[1:2, sl] + nxt * cw_ref[2:3, sl] + cb_ref[:, sl]
        act = gp * _sigmoid(gp) * up
        acc = acc + _mm(act.astype(BF16), wd_ref[sl, :])
    out_ref[...] = acc


def _ffn(x1, g, wg, wu, cw, cb, wd, S):
    T = x1.shape[0]
    tm = TM_FFN
    consts = [g, wg, wu, cw, cb, wd]
    return pl.pallas_call(
        functools.partial(_ffn_kernel, tiles_per_seq=S // tm),
        grid=(T // tm,),
        in_specs=[pl.BlockSpec((tm, D_MODEL), lambda i: (i, 0))] + _halo_specs(tm, D_MODEL, T)
        + [_const_spec(c.shape) for c in consts],
        out_specs=pl.BlockSpec((tm, D_MODEL), lambda i: (i, 0)),
        out_shape=jax.ShapeDtypeStruct((T, D_MODEL), F32),
        compiler_params=_params(("parallel",)),
        name="ffn",
    )(x1, x1, x1, *consts)


def _pad_cols(w, width):
    return jnp.pad(w, ((0, 0), (0, width - w.shape[1])))


def _pad_rows(w, height):
    return jnp.pad(w, ((0, height - w.shape[0]), (0, 0)))


def _rw_layout(t):
    W = RWKV_WIDTH
    pad = lambda a: jnp.pad(a, [(0, 0)] * (a.ndim - 1) + [(0, LANE - a.shape[-1])])
    o = 3 * W
    return jnp.concatenate([t[..., 0:o], pad(t[..., o:o + 64]), pad(t[..., o + 64:o + 128]),
                            pad(t[..., o + 128:o + 192]), t[..., o + 192:o + 448]], axis=-1)


def kernel(x, positions, norm_mix_g, w_in, b_gate, q_a_norm_g, kv_a_norm_g, w_uq, w_ukv, qn_norm_g, qr_norm_g,
           kn_norm_g, kr_norm_g, shift_mu, w0, w2, a0, a2, g2, k_k, k_a, r_k, ln_x_g, ln_x_b, w_o, w_merge,
           norm_ffn_g, w_ffn_gate, w_ffn_up, ffn_conv_w, ffn_conv_b, w_ffn_down):
    B, S, D = x.shape
    T = B * S
    depth = norm_mix_g.shape[0]
    H = MLA_HEADS
    mla_cols = Q_LORA + KV_LORA + QK_ROPE
    rw_cols = 3 * RWKV_WIDTH + A_LORA + 2 * DECAY_LORA + 2 * GATE_LORA

    half = QK_ROPE // 2
    inv_freq = ROPE_THETA ** (-jnp.arange(0, QK_ROPE, 2, dtype=F32) / QK_ROPE)
    freq = inv_freq.reshape(half, 1)
    fr = jnp.arange(half)
    place = jnp.zeros((3, half, LANE), F32)
    place = place.at[0, fr, QK_NOPE + fr].set(1.0).at[0, fr, QK_NOPE + half + fr].set(1.0)
    place = place.at[1, fr, QK_NOPE + fr].set(-1.0)
    place = place.at[2, fr, QK_NOPE + half + fr].set(1.0)
    place = place.astype(BF16)
    lane = jnp.arange(LANE)
    base = ((lane < QK_NOPE) | (lane >= QK_NOPE + QK_ROPE)).astype(F32).reshape(1, LANE)
    lane2 = jnp.arange(2 * LANE)
    seg_of = 2 * (lane2 // LANE) + (lane2 % LANE >= QK_NOPE)
    seg = (seg_of[:, None] == seg_of[None, :]).astype(BF16)
    invn = jnp.where(lane2 % LANE < QK_NOPE, 1.0 / QK_NOPE, 1.0 / QK_ROPE).astype(F32).reshape(1, 2 * LANE)
    head_of = jnp.arange(GW) // RWKV_HEAD
    bo = (head_of[:, None] == head_of[None, :]).astype(BF16)
    pos_row = positions.reshape(1, T)
    row = lambda v: v.reshape(1, -1).astype(F32)
    scale = math.log2(math.e) / math.sqrt(QK_NOPE + QK_ROPE)

    x2 = x.reshape(T, D)
    for l in range(depth):
        wi = w_in[l]
        w_mla = jnp.concatenate([wi[:, 0:Q_LORA + KV_LORA], jnp.zeros((D, QK_NOPE), F32),
                                 wi[:, Q_LORA + KV_LORA:mla_cols], jnp.zeros((D, LANE - QK_NOPE - QK_ROPE), F32)], axis=1)
        w_rw = _rw_layout(wi[:, mla_cols:mla_cols + rw_cols])
        w_gate = wi[:, mla_cols + rw_cols:]
        wq = _pad_cols(w_uq[l].reshape(Q_LORA, H, QK_NOPE + QK_ROPE).transpose(1, 0, 2).reshape(H * Q_LORA, -1), LANE)
        wq = wq.reshape(H // 2, 2, Q_LORA, LANE).transpose(0, 2, 1, 3).reshape(H // 2, Q_LORA, 2 * LANE).astype(BF16)
        wkv = w_ukv[l].reshape(KV_LORA, H, QK_NOPE + V_HEAD)
        wk = _pad_cols(wkv[:, :, :QK_NOPE].transpose(1, 0, 2).reshape(H * KV_LORA, QK_NOPE), LANE)
        wk = wk.reshape(H // 2, 2, KV_LORA, LANE).transpose(0, 2, 1, 3).reshape(H // 2, KV_LORA, 2 * LANE).astype(BF16)
        wvt = jnp.pad(wkv[:, :, QK_NOPE:], ((0, 0), (0, 0), (0, V_ROWS - V_HEAD))).reshape(KV_LORA, H * V_ROWS).T.astype(BF16)
        one = jnp.tile((jnp.arange(V_ROWS) == V_HEAD).astype(F32), H).reshape(H * V_ROWS, 1)
        gqh = jnp.tile(_pad_cols(jnp.concatenate([qn_norm_g[l], qr_norm_g[l]]).reshape(1, -1), LANE) * scale, (1, 2))
        gkn = jnp.tile(_pad_cols(kn_norm_g[l].reshape(1, -1), LANE), (1, 2))
        gkr = _pad_cols(jnp.concatenate([jnp.zeros((QK_NOPE,), F32), kr_norm_g[l]]).reshape(1, -1), LANE)
        mu = _rw_layout(shift_mu[l])
        mu = jnp.concatenate([mu, 1.0 - mu[0:1] - mu[1:2]], axis=0)
        a2p = _pad_rows(a2[l], LANE).astype(BF16)
        w2p = jnp.pad(w2[l], ((0, 0), (0, LANE - DECAY_LORA), (0, 0))).astype(BF16)
        consts = [row(norm_mix_g[l]), w_mla.astype(BF16), w_rw.astype(BF16), w_gate.astype(BF16),
                  b_gate[l].reshape(1, 2 * D),
                  row(q_a_norm_g[l]), row(kv_a_norm_g[l]), wq, wk, wvt, gqh, gkn, gkr, freq, place, base, one, seg, invn,
                  mu, row(a0[l]), a2p, w0[l], w2p, g2[l].astype(BF16), row(k_k[l]), row(k_a[l]), row(r_k[l]), bo]

        gates, q, k, vt, scan, post = _front(x2, pos_row, consts, B, S)
        oat = _attn(q, k, vt)
        o_f, o_b = _wkv(scan, B, S)
        wo = w_o[l].astype(BF16)
        x2 = _mix(o_f, o_b, post, oat, gates, x2, bo, row(ln_x_g[l]), row(ln_x_b[l]), wo[:H * V_HEAD],
                  wo[H * V_HEAD:], w_merge[l].astype(BF16), B, S)
        x2 = _ffn(x2, row(norm_ffn_g[l]), w_ffn_gate[l].astype(BF16), w_ffn_up[l].astype(BF16), ffn_conv_w[l],
                  ffn_conv_b[l].reshape(1, -1), w_ffn_down[l].astype(BF16), S)
    return x2.reshape(B, S, D)
```

```python
import functools
import math

import jax
import jax.numpy as jnp
from jax import lax
from jax.experimental import pallas as pl
from jax.experimental.pallas import tpu as pltpu

F32 = jnp.float32
BF16 = jnp.bfloat16

D_MODEL = 1024
MLA_HEADS = 8
QK_NOPE = 64
QK_ROPE = 32
V_HEAD = 64
Q_LORA = 256
KV_LORA = 128
ROPE_THETA = 10000.0
RWKV_HEADS = 8
RWKV_HEAD = 64
RWKV_WIDTH = RWKV_HEADS * RWKV_HEAD
DECAY_LORA = 64
A_LORA = 64
GATE_LORA = 128
GN_EPS = 64e-5
FFN_HIDDEN = 2816
NORM_EPS = 1e-6

LANE = 128
SUBLANE = 8
MXU_COLS = 256
VMEM_LIMIT = 56 * 1024 * 1024

V_ROWS = V_HEAD + 16
MLA_W = 512
RW_W = 3 * RWKV_WIDTH + 5 * LANE

CHUNK = 64
GROUP = 4
GW = GROUP * RWKV_HEAD
assert CHUNK == RWKV_HEAD

_seg = lambda i: slice(i * RWKV_WIDTH, (i + 1) * RWKV_WIDTH)
SCAN_R, SCAN_K, SCAN_V, SCAN_KK, SCAN_BV = (_seg(i) for i in range(5))
SCAN_LW = (_seg(5), _seg(6))
SCAN_W = 7 * RWKV_WIDTH
POST_BONUS = _seg(0)
POST_GATE = (_seg(1), _seg(2))
POST_W = 3 * RWKV_WIDTH

TM_FRONT = 512
GATE_CHUNK = MXU_COLS
TQ_ATTN = 2048
TK_ATTN = 256
QK_SPAN = 2048
ATTN_AHEAD = 1
NB_WKV = 4
TM_MIX = 512
TM_FFN = 512
TH_FFN = 1408


def _mm(a, b):
    return jnp.dot(a, b, preferred_element_type=F32)


def _mm_nt(a, b):
    return lax.dot_general(a, b, (((1,), (1,)), ((), ())), preferred_element_type=F32)


def _mm_tn(a, b):
    return lax.dot_general(a, b, (((0,), (0,)), ((), ())), preferred_element_type=F32)


def _split(x):
    hi = x.astype(BF16)
    lo = (x - hi.astype(F32)).astype(BF16)
    return hi, lo


def _head_sums(x, bo):
    xb = x.astype(BF16)
    return jnp.concatenate([_mm(xb[:, c:c + GW], bo) for c in range(0, x.shape[1], GW)], axis=1)


def _sigmoid(x):
    return 0.5 * jnp.tanh(0.5 * x) + 0.5


def _rms(x, g, eps=NORM_EPS):
    return x * lax.rsqrt(jnp.mean(x * x, axis=-1, keepdims=True) + eps) * g


def _const_spec(shape):
    nd = len(shape)
    return pl.BlockSpec(shape, lambda *_: (0,) * nd, pipeline_mode=pl.Buffered(1))


def _params(sem):
    return pltpu.CompilerParams(dimension_semantics=sem, vmem_limit_bytes=VMEM_LIMIT)


def _run_in_turn(streams):
    streams = list(streams)
    while streams:
        for st in list(streams):
            if next(st, "done") == "done":
                streams.remove(st)


def _shifted(p, before, after):
    tm = p.shape[0]
    row = lax.broadcasted_iota(jnp.int32, p.shape, 0)
    prev = jnp.where(row == 0, before, pltpu.roll(p, 1, axis=0))
    nxt = jnp.where(row == tm - 1, after, pltpu.roll(p, tm - 1, axis=0))
    return prev, nxt


def _halo_specs(tm, width, T):
    per = tm // SUBLANE
    before = pl.BlockSpec((SUBLANE, width), lambda i: (jnp.maximum(i * per - 1, 0), 0))
    after = pl.BlockSpec((SUBLANE, width), lambda i: (jnp.minimum((i + 1) * per, T // SUBLANE - 1), 0))
    return [before, after]


def _with_neighbours(x, before_ref, after_ref, tiles_per_seq):
    j = pl.program_id(0) % tiles_per_seq
    before = jnp.where(j == 0, 0.0, before_ref[SUBLANE - 1:SUBLANE, :])
    after = jnp.where(j == tiles_per_seq - 1, 0.0, after_ref[0:1, :])
    width = x.shape[1]
    return jnp.concatenate([x, jnp.broadcast_to(before, (SUBLANE, width)), jnp.broadcast_to(after, (SUBLANE, width))],
                           axis=0)


def _rotary(t, cos, sin_up, sin_dn):
    up = pltpu.roll(t, LANE - QK_ROPE // 2, axis=1)
    dn = pltpu.roll(t, QK_ROPE // 2, axis=1)
    return t * cos + up * sin_up + dn * sin_dn


def _front_kernel(x_ref, xb_ref, xa_ref, pos_ref, gmix_ref, wm_ref, wr_ref, wg_ref, bg_ref,
                  gq_ref, gkv_ref, wq_ref, wk_ref, wvt_ref, gqh_ref, gkn_ref, gkr_ref, freq_ref, place_ref, base_ref,
                  one_ref, seg_ref, invn_ref,
                  mu_ref, a0_ref, a2_ref, w0_ref, w2_ref, g2_ref, kk_ref, ka_ref, rk_ref, bo_ref,
                  gate_out, q_out, k_out, vt_out,
                  scan_out, post_out, *, tiles_per_seq):
    tm = x_ref.shape[0]
    gmix = gmix_ref[...]
    h_all = _rms(_with_neighbours(x_ref[...], xb_ref, xa_ref, tiles_per_seq), gmix).astype(BF16)
    h = h_all[0:tm]

    def gates():
        for c in range(0, 2 * D_MODEL, GATE_CHUNK):
            sl = slice(c, c + GATE_CHUNK)
            gate_out[:, sl] = _sigmoid(_mm(h, wg_ref[:, sl]) + bg_ref[:, sl]).astype(gate_out.dtype)
            yield

    def rwkv():
        p_all = _mm(h_all, wr_ref[...])
        yield
        p = p_all[0:tm]
        prev, nxt = _shifted(p, p_all[tm:tm + 1, :], p_all[tm + SUBLANE:tm + SUBLANE + 1, :])
        u = p * mu_ref[2:3, :] + prev * mu_ref[0:1, :] + nxt * mu_ref[1:2, :]
        W = RWKV_WIDTH
        r, k, v = u[:, 0:W], u[:, W:2 * W], u[:, 2 * W:3 * W]
        scan_out[:, SCAN_R] = r
        scan_out[:, SCAN_V] = v
        yield
        a = _sigmoid(a0_ref[...] + _mm(u[:, 3 * W:3 * W + LANE].astype(BF16), a2_ref[...]))
        bo = bo_ref[...]
        kkf = k * kk_ref[...]
        ss = _head_sums(kkf * kkf, bo)
        yield
        kkn = kkf * jnp.where(ss >= 1e-24, lax.rsqrt(ss), 1e12)
        k = k * (1.0 + (a - 1.0) * ka_ref[...])
        scan_out[:, SCAN_K] = k
        scan_out[:, SCAN_KK] = kkn
        scan_out[:, SCAN_BV] = kkn * a
        yield
        post_out[:, POST_BONUS] = (_head_sums(r * k * rk_ref[...], bo) * v).astype(post_out.dtype)
        yield
        for d in range(2):
            dlo = u[:, 3 * W + (1 + d) * LANE:3 * W + (2 + d) * LANE]
            glo = u[:, 3 * W + (3 + d) * LANE:3 * W + (4 + d) * LANE]
            z = -(w0_ref[d:d + 1, :] + _mm(jnp.tanh(dlo).astype(BF16), w2_ref[d]))
            softplus = jnp.maximum(z, 0.0) + jnp.log(1.0 + jnp.exp(-jnp.abs(z)))
            scan_out[:, SCAN_LW[d]] = -jnp.exp(-softplus - 0.5)
            yield
            post_out[:, POST_GATE[d]] = _mm(_sigmoid(glo).astype(BF16), g2_ref[d]).astype(post_out.dtype)
            yield

    def mla():
        lat = _mm(h, wm_ref[...])
        ang = freq_ref[...] * pos_ref[...].astype(F32)
        place = place_ref[...]
        cos_hl = jnp.concatenate(_split(jnp.cos(ang)), axis=0)
        sin_hl = jnp.concatenate(_split(jnp.sin(ang)), axis=0)
        yield
        cos = _mm_tn(cos_hl, jnp.concatenate([place[0], place[0]], axis=0)) + base_ref[...]
        sin_up = _mm_tn(sin_hl, jnp.concatenate([place[1], place[1]], axis=0))
        sin_dn = _mm_tn(sin_hl, jnp.concatenate([place[2], place[2]], axis=0))
        seg, invn = seg_ref[...], invn_ref[...]

        def head_norm(t, gain):
            ss = _mm((t * t).astype(BF16), seg[0:t.shape[1], 0:t.shape[1]])
            return t * lax.rsqrt(ss * invn[:, 0:t.shape[1]] + NORM_EPS) * gain

        cq = _rms(lat[:, 0:Q_LORA], gq_ref[...]).astype(BF16)
        ckv = _rms(lat[:, Q_LORA:Q_LORA + KV_LORA], gkv_ref[...]).astype(BF16)
        yield
        kr = _rotary(head_norm(lat[:, Q_LORA + KV_LORA:MLA_W], gkr_ref[...]), cos, sin_up, sin_dn)
        vt_out[0] = (_mm_nt(wvt_ref[...], ckv) + one_ref[...]).astype(BF16)
        yield
        for pair in range(MLA_HEADS // 2):
            q2 = head_norm(_mm(cq, wq_ref[pair]), gqh_ref[...])
            for i in range(2):
                q_out[0, 2 * pair + i] = _rotary(q2[:, i * LANE:(i + 1) * LANE], cos, sin_up, sin_dn).astype(BF16)
            yield
            k2 = head_norm(_mm(ckv, wk_ref[pair]), gkn_ref[...])
            for i in range(2):
                k_out[0, 2 * pair + i] = (k2[:, i * LANE:(i + 1) * LANE] + kr).astype(BF16)
            yield

    _run_in_turn([gates(), rwkv(), mla()])


def _front(x2, pos_row, consts, B, S):
    T = x2.shape[0]
    tm = TM_FRONT
    nj = S // tm
    H = MLA_HEADS
    tok = lambda w: pl.BlockSpec((tm, w), lambda i: (i, 0))
    heads = pl.BlockSpec((1, H, tm, LANE), lambda i: (i // nj, 0, i % nj, 0))
    return pl.pallas_call(
        functools.partial(_front_kernel, tiles_per_seq=nj),
        grid=(T // tm,),
        in_specs=[tok(D_MODEL)] + _halo_specs(tm, D_MODEL, T) + [pl.BlockSpec((1, tm), lambda i: (0, i))]
        + [_const_spec(c.shape) for c in consts],
        out_specs=[tok(2 * D_MODEL), heads, heads, pl.BlockSpec((1, H * V_ROWS, tm), lambda i: (i // nj, 0, i % nj))]
        + [tok(SCAN_W), tok(POST_W)],
        out_shape=[jax.ShapeDtypeStruct((T, 2 * D_MODEL), BF16), jax.ShapeDtypeStruct((B, H, S, LANE), BF16),
                   jax.ShapeDtypeStruct((B, H, S, LANE), BF16), jax.ShapeDtypeStruct((B, H * V_ROWS, S), BF16),
                   jax.ShapeDtypeStruct((T, SCAN_W), F32), jax.ShapeDtypeStruct((T, POST_W), BF16)],
        compiler_params=_params(("parallel",)),
        name="front",
    )(x2, x2, x2, pos_row, *consts)


def _attn_kernel(q_ref, k_ref, vt_ref, ot_ref):
    S = k_ref.shape[2]
    tk = TK_ATTN
    subs = [q_ref[0, 0, i:i + MXU_COLS, :] for i in range(0, q_ref.shape[2], MXU_COLS)]
    per_span = QK_SPAN // tk
    nspan = S // QK_SPAN

    def scores_for(g):
        kb = k_ref[0, 0, g * QK_SPAN:(g + 1) * QK_SPAN, :]
        return [_mm_nt(kb, q) for q in subs]

    m = [jnp.full((1, MXU_COLS), -jnp.inf, F32) for _ in subs]
    acc = [jnp.zeros((V_ROWS, MXU_COLS), F32) for _ in subs]
    scores = [scores_for(g) for g in range(min(ATTN_AHEAD, nspan))]
    for g in range(nspan):
        if g + ATTN_AHEAD < nspan:
            scores.append(scores_for(g + ATTN_AHEAD))
        for jj in range(per_span):
            j = g * per_span + jj
            vb = vt_ref[0, :, j * tk:(j + 1) * tk]
            for i, s_span in enumerate(scores[g]):
                s = s_span[jj * tk:(jj + 1) * tk]
                m_new = jnp.maximum(m[i], jnp.max(s, axis=0, keepdims=True))
                p = jnp.exp2(s - m_new)
                alpha = jnp.exp2(m[i] - m_new)
                acc[i] = alpha * acc[i] + _mm(vb, p.astype(BF16))
                m[i] = m_new
        scores[g] = None
    for i in range(len(subs)):
        ot_ref[0, :, i * MXU_COLS:(i + 1) * MXU_COLS] = (acc[i][0:V_HEAD] / acc[i][V_HEAD:V_HEAD + 1]).astype(BF16)


def _attn(q, k, vt):
    B, H, S, _ = q.shape
    tq = TQ_ATTN
    return pl.pallas_call(
        _attn_kernel,
        grid=(B, H, S // tq),
        in_specs=[pl.BlockSpec((1, 1, tq, LANE), lambda b, h, i: (b, h, i, 0)),
                  pl.BlockSpec((1, 1, S, LANE), lambda b, h, i: (b, h, 0, 0)),
                  pl.BlockSpec((1, V_ROWS, S), lambda b, h, i: (b, h, 0))],
        out_specs=pl.BlockSpec((1, V_HEAD, tq), lambda b, h, i: (b, h, i)),
        out_shape=jax.ShapeDtypeStruct((B, H * V_HEAD, S), BF16),
        compiler_params=_params(("parallel", "parallel", "parallel")),
        name="attn",
    )(q, k, vt)


def _wkv_chain(r, k, v, kk, bv, lw, g, reverse):
    C = CHUNK
    sgn = -1 if reverse else 1
    rc = lax.broadcasted_iota(jnp.int32, (C, C), 0)
    cc = lax.broadcasted_iota(jnp.int32, (C, C), 1)
    tri = jnp.where((rc - cc) * sgn >= 0, 1.0, 0.0).astype(BF16)
    lw_hi, lw_lo = _split(lw)
    cum = _mm(tri, lw_hi) + _mm(tri, lw_lo)
    total = jnp.sum(lw, axis=0, keepdims=True)
    e_neg = jnp.exp(-cum)
    e_end = jnp.exp(total - cum)

    row = lax.broadcasted_iota(jnp.int32, (C, GW), 0)
    lane = lax.broadcasted_iota(jnp.int32, (C, GW), 1)
    lane_head = lane // RWKV_HEAD
    ahead = (row - lane % RWKV_HEAD) * sgn
    strict = ahead > 0
    incl = ahead >= 0

    def diag(x):
        xb = x.astype(BF16)
        return jnp.concatenate([jnp.where(lane_head == h, xb, jnp.zeros_like(xb)) for h in range(GROUP)], axis=0)

    def undiag(x):
        return sum(jnp.where(lane_head == h, x[h * C:(h + 1) * C], 0.0) for h in range(GROUP))

    a_t = (-kk * jnp.exp(cum - lw)).astype(BF16)
    r_t = r * jnp.exp(cum)
    dg_v = diag(v)

    yield
    pm = _mm_nt(jnp.concatenate([a_t, r_t.astype(BF16)], axis=0),
                jnp.concatenate([diag(bv * e_neg), diag(k * e_neg)], axis=0))
    n_pow = jnp.where(strict, pm[0:C, 0:GW], 0.0)
    m_ak = jnp.where(strict, pm[0:C, GW:2 * GW], 0.0)
    m_rb = jnp.where(incl, pm[C:2 * C, 0:GW], 0.0).astype(BF16)
    m_rk = jnp.where(incl, pm[C:2 * C, GW:2 * GW], 0.0)
    yield
    wo = _mm(jnp.concatenate([m_ak, m_rk], axis=0).astype(BF16), dg_v)
    w1, o0 = wo[0:C], wo[C:2 * C]

    levels = int(math.log2(C))
    tinv = jnp.where(ahead == 0, 1.0, n_pow)
    for lvl in range(levels):
        yield
        lhs = [] if lvl == levels - 1 else [n_pow]
        lhs += [] if lvl == 0 else [tinv]
        res = _mm(jnp.concatenate(lhs, axis=0).astype(BF16), diag(n_pow))
        if lvl > 0:
            tinv = tinv + res[(len(lhs) - 1) * C:len(lhs) * C]
        n_pow = res[0:C]

    yield
    au = _mm(tinv.astype(BF16), jnp.concatenate([diag(a_t), diag(w1)], axis=1))
    a2, u0 = au[:, 0:GW], au[:, GW:2 * GW]
    yield
    ro = _mm(m_rb, jnp.concatenate([diag(a2), diag(u0)], axis=1))
    r2 = r_t + ro[:, 0:GW]
    o0 = o0 + ro[:, GW:2 * GW]
    yield
    ends = jnp.concatenate([bv * e_end, k * e_end], axis=0).astype(BF16)
    vals = jnp.concatenate([jnp.concatenate([a2, u0], axis=1),
                            jnp.concatenate([jnp.zeros_like(v), v], axis=1)], axis=0).astype(BF16)
    pp = _mm_tn(ends, vals)
    phi, psi = undiag(pp[:, 0:GW]), undiag(pp[:, GW:2 * GW])
    yield
    res = _mm(jnp.concatenate([r2, phi], axis=0).astype(BF16), diag(g))
    on_diag = jnp.where(ahead == 0, jnp.exp(total), 0.0)
    p_end = sum(jnp.where(lane_head == h, jnp.sum(jnp.where(lane_head == h, on_diag, 0.0), axis=1, keepdims=True), 0.0)
                for h in range(GROUP))
    yield res[0:C] + o0, p_end * g + res[C:2 * C] + psi


def _wkv_kernel(fwd_ref, bwd_ref, of_ref, ob_ref, g_ref):
    @pl.when(pl.program_id(1) == 0)
    def _():
        g_ref[...] = jnp.zeros_like(g_ref)

    chains = []
    for bi in range(of_ref.shape[0]):
        for d, (in_ref, o_ref) in enumerate(((fwd_ref, of_ref), (bwd_ref, ob_ref))):
            for gidx in range(RWKV_WIDTH // GW):
                sl = slice(gidx * GW, (gidx + 1) * GW)
                ins = [in_ref[bi, :, seg.start + sl.start:seg.start + sl.stop]
                       for seg in (SCAN_R, SCAN_K, SCAN_V, SCAN_KK, SCAN_BV, SCAN_LW[d])]
                gen = _wkv_chain(*ins, g_ref[bi, d, gidx], reverse=d == 1)
                chains.append((gen, o_ref, bi, sl, d, gidx))
    while chains:
        for chain in list(chains):
            gen, o_ref, bi, sl, d, gidx = chain
            out = next(gen)
            if out is not None:
                o_ref[bi, :, sl] = out[0].astype(o_ref.dtype)
                g_ref[bi, d, gidx] = out[1]
                chains.remove(chain)


def _wkv(scan, B, S):
    T = scan.shape[0]
    W = RWKV_WIDTH
    C = CHUNK
    nb = NB_WKV
    nc = S // C
    blk = lambda w, rev: pl.BlockSpec((nb, C, w), lambda b, c: (b, nc - 1 - c if rev else c, 0))
    out = jax.ShapeDtypeStruct((B, S, W), BF16)
    scan = scan.reshape(B, S, SCAN_W)
    o_f, o_b = pl.pallas_call(
        _wkv_kernel,
        grid=(B // nb, nc),
        in_specs=[blk(SCAN_W, False), blk(SCAN_W, True)],
        out_specs=[blk(W, False), blk(W, True)],
        out_shape=[out, out],
        scratch_shapes=[pltpu.VMEM((nb, 2, W // GW, RWKV_HEAD, GW), F32)],
        compiler_params=_params(("parallel", "arbitrary")),
        name="wkv",
    )(scan, scan)
    return o_f.reshape(T, W), o_b.reshape(T, W)


def _mix_kernel(of_ref, ob_ref, post_ref, oat_ref, gate_ref, x_ref, bo_ref, lng_ref, lnb_ref,
                woa_ref, wob_ref, wm_ref, out_ref):
    bo = bo_ref[...]
    bonus = post_ref[:, POST_BONUS].astype(F32)
    ob = None
    for d, o_ref in enumerate((of_ref, ob_ref)):
        o = o_ref[...].astype(F32)
        mu = _head_sums(o, bo) / RWKV_HEAD
        dl = o - mu
        var = _head_sums(dl * dl, bo) / RWKV_HEAD
        y = dl * lax.rsqrt(var + GN_EPS) * lng_ref[...] + lnb_ref[...] + bonus
        y = y * post_ref[:, POST_GATE[d]]
        ob = y if ob is None else ob + y
    y_b = _mm(ob.astype(BF16), wob_ref[...])
    y_a = _mm_tn(oat_ref[0], woa_ref[...])
    mixed = gate_ref[:, 0:D_MODEL] * y_a + gate_ref[:, D_MODEL:2 * D_MODEL] * y_b
    out_ref[...] = x_ref[...] + _mm(mixed.astype(BF16), wm_ref[...])


def _mix(o_f, o_b, post, oat, gates, x2, bo, lng, lnb, woa, wob, wm, B, S):
    T = x2.shape[0]
    tm = TM_MIX
    nj = S // tm
    consts = [bo, lng, lnb, woa, wob, wm]
    tok = lambda w: pl.BlockSpec((tm, w), lambda b, j: (b * nj + j, 0))
    return pl.pallas_call(
        _mix_kernel,
        grid=(B, nj),
        in_specs=[tok(RWKV_WIDTH), tok(RWKV_WIDTH), tok(POST_W),
                  pl.BlockSpec((1, MLA_HEADS * V_HEAD, tm), lambda b, j: (b, 0, j)),
                  pl.BlockSpec((tm, 2 * D_MODEL), lambda b, j: (b * nj + j, 0)),
                  pl.BlockSpec((tm, D_MODEL), lambda b, j: (b * nj + j, 0))]
        + [_const_spec(c.shape) for c in consts],
        out_specs=pl.BlockSpec((tm, D_MODEL), lambda b, j: (b * nj + j, 0)),
        out_shape=jax.ShapeDtypeStruct((T, D_MODEL), F32),
        compiler_params=_params(("parallel", "parallel")),
        name="mix",
    )(o_f, o_b, post, oat, gates, x2, *consts)


def _ffn_kernel(x_ref, before_ref, after_ref, g_ref, wg_ref, wu_ref, cw_ref, cb_ref, wd_ref, out_ref, *, tiles_per_seq):
    tm = x_ref.shape[0]
    x = x_ref[...]
    h_all = _rms(_with_neighbours(x, before_ref, after_ref, tiles_per_seq), g_ref[...]).astype(BF16)
    h = h_all[0:tm]
    chunks = [slice(c, c + TH_FFN) for c in range(0, FFN_HIDDEN, TH_FFN)]
    pres = [_mm(h_all, wg_ref[:, sl]) for sl in chunks]
    ups = [_mm(h, wu_ref[:, sl]) for sl in chunks]
    acc = x
    for sl, pre_all, up in zip(chunks, pres, ups):
        pre = pre_all[0:tm]
        prev, nxt = _shifted(pre, pre_all[tm:tm + 1, :], pre_all[tm + SUBLANE:tm + SUBLANE + 1, :])
        gp = prev * cw_ref[0:1, sl] + pre * cw_ref[1:2, sl] + nxt * cw_ref[2:3, sl] + cb_ref[:, sl]
        act = gp * _sigmoid(gp) * up
        acc = acc + _mm(act.astype(BF16), wd_ref[sl, :])
    out_ref[...] = acc


def _ffn(x1, g, wg, wu, cw, cb, wd, S):
    T = x1.shape[0]
    tm = TM_FFN
    consts = [g, wg, wu, cw, cb, wd]
    return pl.pallas_call(
        functools.partial(_ffn_kernel, tiles_per_seq=S // tm),
        grid=(T // tm,),
        in_specs=[pl.BlockSpec((tm, D_MODEL), lambda i: (i, 0))] + _halo_specs(tm, D_MODEL, T)
        + [_const_spec(c.shape) for c in consts],
        out_specs=pl.BlockSpec((tm, D_MODEL), lambda i: (i, 0)),
        out_shape=jax.ShapeDtypeStruct((T, D_MODEL), F32),
        compiler_params=_params(("parallel",)),
        name="ffn",
    )(x1, x1, x1, *consts)


def _pad_cols(w, width):
    return jnp.pad(w, ((0, 0), (0, width - w.shape[1])))


def _pad_rows(w, height):
    return jnp.pad(w, ((0, height - w.shape[0]), (0, 0)))


def _rw_layout(t):
    W = RWKV_WIDTH
    pad = lambda a: jnp.pad(a, [(0, 0)] * (a.ndim - 1) + [(0, LANE - a.shape[-1])])
    o = 3 * W
    return jnp.concatenate([t[..., 0:o], pad(t[..., o:o + 64]), pad(t[..., o + 64:o + 128]),
                            pad(t[..., o + 128:o + 192]), t[..., o + 192:o + 448]], axis=-1)


def kernel(x, positions, norm_mix_g, w_in, b_gate, q_a_norm_g, kv_a_norm_g, w_uq, w_ukv, qn_norm_g, qr_norm_g,
           kn_norm_g, kr_norm_g, shift_mu, w0, w2, a0, a2, g2, k_k, k_a, r_k, ln_x_g, ln_x_b, w_o, w_merge,
           norm_ffn_g, w_ffn_gate, w_ffn_up, ffn_conv_w, ffn_conv_b, w_ffn_down):
    B, S, D = x.shape
    T = B * S
    depth = norm_mix_g.shape[0]
    H = MLA_HEADS
    mla_cols = Q_LORA + KV_LORA + QK_ROPE
    rw_cols = 3 * RWKV_WIDTH + A_LORA + 2 * DECAY_LORA + 2 * GATE_LORA

    half = QK_ROPE // 2
    inv_freq = ROPE_THETA ** (-jnp.arange(0, QK_ROPE, 2, dtype=F32) / QK_ROPE)
    freq = inv_freq.reshape(half, 1)
    fr = jnp.arange(half)
    place = jnp.zeros((3, half, LANE), F32)
    place = place.at[0, fr, QK_NOPE + fr].set(1.0).at[0, fr, QK_NOPE + half + fr].set(1.0)
    place = place.at[1, fr, QK_NOPE + fr].set(-1.0)
    place = place.at[2, fr, QK_NOPE + half + fr].set(1.0)
    place = place.astype(BF16)
    lane = jnp.arange(LANE)
    base = ((lane < QK_NOPE) | (lane >= QK_NOPE + QK_ROPE)).astype(F32).reshape(1, LANE)
    lane2 = jnp.arange(2 * LANE)
    seg_of = 2 * (lane2 // LANE) + (lane2 % LANE >= QK_NOPE)
    seg = (seg_of[:, None] == seg_of[None, :]).astype(BF16)
    invn = jnp.where(lane2 % LANE < QK_NOPE, 1.0 / QK_NOPE, 1.0 / QK_ROPE).astype(F32).reshape(1, 2 * LANE)
    head_of = jnp.arange(GW) // RWKV_HEAD
    bo = (head_of[:, None] == head_of[None, :]).astype(BF16)
    pos_row = positions.reshape(1, T)
    row = lambda v: v.reshape(1, -1).astype(F32)
    scale = math.log2(math.e) / math.sqrt(QK_NOPE + QK_ROPE)

    x2 = x.reshape(T, D)
    for l in range(depth):
        wi = w_in[l]
        w_mla = jnp.concatenate([wi[:, 0:Q_LORA + KV_LORA], jnp.zeros((D, QK_NOPE), F32),
                                 wi[:, Q_LORA + KV_LORA:mla_cols], jnp.zeros((D, LANE - QK_NOPE - QK_ROPE), F32)], axis=1)
        w_rw = _rw_layout(wi[:, mla_cols:mla_cols + rw_cols])
        w_gate = wi[:, mla_cols + rw_cols:]
        wq = _pad_cols(w_uq[l].reshape(Q_LORA, H, QK_NOPE + QK_ROPE).transpose(1, 0, 2).reshape(H * Q_LORA, -1), LANE)
        wq = wq.reshape(H // 2, 2, Q_LORA, LANE).transpose(0, 2, 1, 3).reshape(H // 2, Q_LORA, 2 * LANE).astype(BF16)
        wkv = w_ukv[l].reshape(KV_LORA, H, QK_NOPE + V_HEAD)
        wk = _pad_cols(wkv[:, :, :QK_NOPE].transpose(1, 0, 2).reshape(H * KV_LORA, QK_NOPE), LANE)
        wk = wk.reshape(H // 2, 2, KV_LORA, LANE).transpose(0, 2, 1, 3).reshape(H // 2, KV_LORA, 2 * LANE).astype(BF16)
        wvt = jnp.pad(wkv[:, :, QK_NOPE:], ((0, 0), (0, 0), (0, V_ROWS - V_HEAD))).reshape(KV_LORA, H * V_ROWS).T.astype(BF16)
        one = jnp.tile((jnp.arange(V_ROWS) == V_HEAD).astype(F32), H).reshape(H * V_ROWS, 1)
        gqh = jnp.tile(_pad_cols(jnp.concatenate([qn_norm_g[l], qr_norm_g[l]]).reshape(1, -1), LANE) * scale, (1, 2))
        gkn = jnp.tile(_pad_cols(kn_norm_g[l].reshape(1, -1), LANE), (1, 2))
        gkr = _pad_cols(jnp.concatenate([jnp.zeros((QK_NOPE,), F32), kr_norm_g[l]]).reshape(1, -1), LANE)
        mu = _rw_layout(shift_mu[l])
        mu = jnp.concatenate([mu, 1.0 - mu[0:1] - mu[1:2]], axis=0)
        a2p = _pad_rows(a2[l], LANE).astype(BF16)
        w2p = jnp.pad(w2[l], ((0, 0), (0, LANE - DECAY_LORA), (0, 0))).astype(BF16)
        consts = [row(norm_mix_g[l]), w_mla.astype(BF16), w_rw.astype(BF16), w_gate.astype(BF16),
                  b_gate[l].reshape(1, 2 * D),
                  row(q_a_norm_g[l]), row(kv_a_norm_g[l]), wq, wk, wvt, gqh, gkn, gkr, freq, place, base, one, seg, invn,
                  mu, row(a0[l]), a2p, w0[l], w2p, g2[l].astype(BF16), row(k_k[l]), row(k_a[l]), row(r_k[l]), bo]

        gates, q, k, vt, scan, post = _front(x2, pos_row, consts, B, S)
        oat = _attn(q, k, vt)
        o_f, o_b = _wkv(scan, B, S)
        wo = w_o[l].astype(BF16)
        x2 = _mix(o_f, o_b, post, oat, gates, x2, bo, row(ln_x_g[l]), row(ln_x_b[l]), wo[:H * V_HEAD],
                  wo[H * V_HEAD:], w_merge[l].astype(BF16), B, S)
        x2 = _ffn(x2, row(norm_ffn_g[l]), w_ffn_gate[l].astype(BF16), w_ffn_up[l].astype(BF16), ffn_conv_w[l],
                  ffn_conv_b[l].reshape(1, -1), w_ffn_down[l].astype(BF16), S)
    return x2.reshape(B, S, D)
```

```python
import functools
import math

import jax
import jax.numpy as jnp
from jax import lax
from jax.experimental import pallas as pl
from jax.experimental.pallas import tpu as pltpu

F32 = jnp.float32
BF16 = jnp.bfloat16

D_MODEL = 1024
MLA_HEADS = 8
QK_NOPE = 64
QK_ROPE = 32
V_HEAD = 64
Q_LORA = 256
KV_LORA = 128
ROPE_THETA = 10000.0
RWKV_HEADS = 8
RWKV_HEAD = 64
RWKV_WIDTH = RWKV_HEADS * RWKV_HEAD
DECAY_LORA = 64
A_LORA = 64
GATE_LORA = 128
GN_EPS = 64e-5
FFN_HIDDEN = 2816
NORM_EPS = 1e-6

LANE = 128
SUBLANE = 8
MXU_COLS = 256
VMEM_LIMIT = 56 * 1024 * 1024

V_ROWS = V_HEAD + 16
MLA_W = 512
RW_W = 3 * RWKV_WIDTH + 5 * LANE

CHUNK = 64
GROUP = 4
GW = GROUP * RWKV_HEAD
assert CHUNK == RWKV_HEAD

_seg = lambda i: slice(i * RWKV_WIDTH, (i + 1) * RWKV_WIDTH)
SCAN_R, SCAN_K, SCAN_V, SCAN_KK, SCAN_BV = (_seg(i) for i in range(5))
SCAN_LW = (_seg(5), _seg(6))
SCAN_W = 7 * RWKV_WIDTH
POST_BONUS = _seg(0)
POST_GATE = (_seg(1), _seg(2))
POST_W = 3 * RWKV_WIDTH

TM_FRONT = 512
GATE_CHUNK = MXU_COLS
TQ_ATTN = 2048
TK_ATTN = 256
QK_SPAN = 2048
ATTN_AHEAD = 1
NB_WKV = 4
TM_MIX = 1024
TM_FFN = 512
TH_FFN = FFN_HIDDEN


def _mm(a, b):
    return jnp.dot(a, b, preferred_element_type=F32)


def _mm_nt(a, b):
    return lax.dot_general(a, b, (((1,), (1,)), ((), ())), preferred_element_type=F32)


def _mm_tn(a, b):
    return lax.dot_general(a, b, (((0,), (0,)), ((), ())), preferred_element_type=F32)


def _split(x):
    hi = x.astype(BF16)
    lo = (x - hi.astype(F32)).astype(BF16)
    return hi, lo


def _head_sums(x, bo):
    xb = x.astype(BF16)
    return jnp.concatenate([_mm(xb[:, c:c + GW], bo) for c in range(0, x.shape[1], GW)], axis=1)


def _sigmoid(x):
    return 0.5 * jnp.tanh(0.5 * x) + 0.5


def _rms(x, g, eps=NORM_EPS):
    return x * lax.rsqrt(jnp.mean(x * x, axis=-1, keepdims=True) + eps) * g


def _const_spec(shape):
    nd = len(shape)
    return pl.BlockSpec(shape, lambda *_: (0,) * nd, pipeline_mode=pl.Buffered(1))


def _params(sem):
    return pltpu.CompilerParams(dimension_semantics=sem, vmem_limit_bytes=VMEM_LIMIT)


def _run_in_turn(streams):
    streams = list(streams)
    while streams:
        for st in list(streams):
            if next(st, "done") == "done":
                streams.remove(st)


def _shifted(p, before, after):
    tm = p.shape[0]
    row = lax.broadcasted_iota(jnp.int32, p.shape, 0)
    prev = jnp.where(row == 0, before, pltpu.roll(p, 1, axis=0))
    nxt = jnp.where(row == tm - 1, after, pltpu.roll(p, tm - 1, axis=0))
    return prev, nxt


def _halo_specs(tm, width, T):
    per = tm // SUBLANE
    before = pl.BlockSpec((SUBLANE, width), lambda i: (jnp.maximum(i * per - 1, 0), 0))
    after = pl.BlockSpec((SUBLANE, width), lambda i: (jnp.minimum((i + 1) * per, T // SUBLANE - 1), 0))
    return [before, after]


def _with_neighbours(x, before_ref, after_ref, tiles_per_seq):
    j = pl.program_id(0) % tiles_per_seq
    before = jnp.where(j == 0, 0.0, before_ref[SUBLANE - 1:SUBLANE, :])
    after = jnp.where(j == tiles_per_seq - 1, 0.0, after_ref[0:1, :])
    width = x.shape[1]
    return jnp.concatenate([x, jnp.broadcast_to(before, (SUBLANE, width)), jnp.broadcast_to(after, (SUBLANE, width))],
                           axis=0)


def _rotary(t, cos, sin_up, sin_dn):
    up = pltpu.roll(t, LANE - QK_ROPE // 2, axis=1)
    dn = pltpu.roll(t, QK_ROPE // 2, axis=1)
    return t * cos + up * sin_up + dn * sin_dn


def _front_kernel(x_ref, xb_ref, xa_ref, pos_ref, gmix_ref, wm_ref, wr_ref, wg_ref, bg_ref,
                  gq_ref, gkv_ref, wq_ref, wk_ref, wvt_ref, gqh_ref, gkn_ref, gkr_ref, freq_ref, place_ref, base_ref,
                  one_ref, seg_ref, invn_ref,
                  mu_ref, a0_ref, a2_ref, w0_ref, w2_ref, g2_ref, kk_ref, ka_ref, rk_ref, bo_ref,
                  gate_out, q_out, k_out, vt_out,
                  scan_out, post_out, *, tiles_per_seq):
    tm = x_ref.shape[0]
    gmix = gmix_ref[...]
    h_all = _rms(_with_neighbours(x_ref[...], xb_ref, xa_ref, tiles_per_seq), gmix).astype(BF16)
    h = h_all[0:tm]

    def gates():
        for c in range(0, 2 * D_MODEL, GATE_CHUNK):
            sl = slice(c, c + GATE_CHUNK)
            gate_out[:, sl] = _sigmoid(_mm(h, wg_ref[:, sl]) + bg_ref[:, sl]).astype(gate_out.dtype)
            yield

    def rwkv():
        p_all = _mm(h_all, wr_ref[...])
        yield
        p = p_all[0:tm]
        prev, nxt = _shifted(p, p_all[tm:tm + 1, :], p_all[tm + SUBLANE:tm + SUBLANE + 1, :])
        u = p * mu_ref[2:3, :] + prev * mu_ref[0:1, :] + nxt * mu_ref[1:2, :]
        W = RWKV_WIDTH
        r, k, v = u[:, 0:W], u[:, W:2 * W], u[:, 2 * W:3 * W]
        scan_out[:, SCAN_R] = r
        scan_out[:, SCAN_V] = v
        yield
        a = _sigmoid(a0_ref[...] + _mm(u[:, 3 * W:3 * W + LANE].astype(BF16), a2_ref[...]))
        bo = bo_ref[...]
        kkf = k * kk_ref[...]
        ss = _head_sums(kkf * kkf, bo)
        yield
        kkn = kkf * jnp.where(ss >= 1e-24, lax.rsqrt(ss), 1e12)
        k = k * (1.0 + (a - 1.0) * ka_ref[...])
        scan_out[:, SCAN_K] = k
        scan_out[:, SCAN_KK] = kkn
        scan_out[:, SCAN_BV] = kkn * a
        yield
        post_out[:, POST_BONUS] = (_head_sums(r * k * rk_ref[...], bo) * v).astype(post_out.dtype)
        yield
        for d in range(2):
            dlo = u[:, 3 * W + (1 + d) * LANE:3 * W + (2 + d) * LANE]
            glo = u[:, 3 * W + (3 + d) * LANE:3 * W + (4 + d) * LANE]
            z = -(w0_ref[d:d + 1, :] + _mm(jnp.tanh(dlo).astype(BF16), w2_ref[d]))
            softplus = jnp.maximum(z, 0.0) + jnp.log(1.0 + jnp.exp(-jnp.abs(z)))
            scan_out[:, SCAN_LW[d]] = -jnp.exp(-softplus - 0.5)
            yield
            post_out[:, POST_GATE[d]] = _mm(_sigmoid(glo).astype(BF16), g2_ref[d]).astype(post_out.dtype)
            yield

    def mla():
        lat = _mm(h, wm_ref[...])
        ang = freq_ref[...] * pos_ref[...].astype(F32)
        place = place_ref[...]
        cos_hl = jnp.concatenate(_split(jnp.cos(ang)), axis=0)
        sin_hl = jnp.concatenate(_split(jnp.sin(ang)), axis=0)
        yield
        cos = _mm_tn(cos_hl, jnp.concatenate([place[0], place[0]], axis=0)) + base_ref[...]
        sin_up = _mm_tn(sin_hl, jnp.concatenate([place[1], place[1]], axis=0))
        sin_dn = _mm_tn(sin_hl, jnp.concatenate([place[2], place[2]], axis=0))
        seg, invn = seg_ref[...], invn_ref[...]

        def head_norm(t, gain):
            ss = _mm((t * t).astype(BF16), seg[0:t.shape[1], 0:t.shape[1]])
            return t * lax.rsqrt(ss * invn[:, 0:t.shape[1]] + NORM_EPS) * gain

        cq = _rms(lat[:, 0:Q_LORA], gq_ref[...]).astype(BF16)
        ckv = _rms(lat[:, Q_LORA:Q_LORA + KV_LORA], gkv_ref[...]).astype(BF16)
        yield
        kr = _rotary(head_norm(lat[:, Q_LORA + KV_LORA:MLA_W], gkr_ref[...]), cos, sin_up, sin_dn)
        vt_out[0] = (_mm_nt(wvt_ref[...], ckv) + one_ref[...]).astype(BF16)
        yield
        for pair in range(MLA_HEADS // 2):
            q2 = head_norm(_mm(cq, wq_ref[pair]), gqh_ref[...])
            for i in range(2):
                q_out[0, 2 * pair + i] = _rotary(q2[:, i * LANE:(i + 1) * LANE], cos, sin_up, sin_dn).astype(BF16)
            yield
            k2 = head_norm(_mm(ckv, wk_ref[pair]), gkn_ref[...])
            for i in range(2):
                k_out[0, 2 * pair + i] = (k2[:, i * LANE:(i + 1) * LANE] + kr).astype(BF16)
            yield

    _run_in_turn([gates(), rwkv(), mla()])


def _front(x2, pos_row, consts, B, S):
    T = x2.shape[0]
    tm = TM_FRONT
    nj = S // tm
    H = MLA_HEADS
    tok = lambda w: pl.BlockSpec((tm, w), lambda i: (i, 0))
    heads = pl.BlockSpec((1, H, tm, LANE), lambda i: (i // nj, 0, i % nj, 0))
    return pl.pallas_call(
        functools.partial(_front_kernel, tiles_per_seq=nj),
        grid=(T // tm,),
        in_specs=[tok(D_MODEL)] + _halo_specs(tm, D_MODEL, T) + [pl.BlockSpec((1, tm), lambda i: (0, i))]
        + [_const_spec(c.shape) for c in consts],
        out_specs=[tok(2 * D_MODEL), heads, heads, pl.BlockSpec((1, H * V_ROWS, tm), lambda i: (i // nj, 0, i % nj))]
        + [tok(SCAN_W), tok(POST_W)],
        out_shape=[jax.ShapeDtypeStruct((T, 2 * D_MODEL), BF16), jax.ShapeDtypeStruct((B, H, S, LANE), BF16),
                   jax.ShapeDtypeStruct((B, H, S, LANE), BF16), jax.ShapeDtypeStruct((B, H * V_ROWS, S), BF16),
                   jax.ShapeDtypeStruct((T, SCAN_W), F32), jax.ShapeDtypeStruct((T, POST_W), BF16)],
        compiler_params=_params(("parallel",)),
        name="front",
    )(x2, x2, x2, pos_row, *consts)


def _attn_kernel(q_ref, k_ref, vt_ref, ot_ref):
    S = k_ref.shape[2]
    tk = TK_ATTN
    subs = [q_ref[0, 0, i:i + MXU_COLS, :] for i in range(0, q_ref.shape[2], MXU_COLS)]
    per_span = QK_SPAN // tk
    nspan = S // QK_SPAN

    def scores_for(g):
        kb = k_ref[0, 0, g * QK_SPAN:(g + 1) * QK_SPAN, :]
        return [_mm_nt(kb, q) for q in subs]

    m = [jnp.full((1, MXU_COLS), -jnp.inf, F32) for _ in subs]
    acc = [jnp.zeros((V_ROWS, MXU_COLS), F32) for _ in subs]
    scores = [scores_for(g) for g in range(min(ATTN_AHEAD, nspan))]
    for g in range(nspan):
        if g + ATTN_AHEAD < nspan:
            scores.append(scores_for(g + ATTN_AHEAD))
        for jj in range(per_span):
            j = g * per_span + jj
            vb = vt_ref[0, :, j * tk:(j + 1) * tk]
            for i, s_span in enumerate(scores[g]):
                s = s_span[jj * tk:(jj + 1) * tk]
                m_new = jnp.maximum(m[i], jnp.max(s, axis=0, keepdims=True))
                p = jnp.exp2(s - m_new)
                alpha = jnp.exp2(m[i] - m_new)
                acc[i] = alpha * acc[i] + _mm(vb, p.astype(BF16))
                m[i] = m_new
        scores[g] = None
    for i in range(len(subs)):
        ot_ref[0, :, i * MXU_COLS:(i + 1) * MXU_COLS] = (acc[i][0:V_HEAD] / acc[i][V_HEAD:V_HEAD + 1]).astype(BF16)


def _attn(q, k, vt):
    B, H, S, _ = q.shape
    tq = TQ_ATTN
    return pl.pallas_call(
        _attn_kernel,
        grid=(B, H, S // tq),
        in_specs=[pl.BlockSpec((1, 1, tq, LANE), lambda b, h, i: (b, h, i, 0)),
                  pl.BlockSpec((1, 1, S, LANE), lambda b, h, i: (b, h, 0, 0)),
                  pl.BlockSpec((1, V_ROWS, S), lambda b, h, i: (b, h, 0))],
        out_specs=pl.BlockSpec((1, V_HEAD, tq), lambda b, h, i: (b, h, i)),
        out_shape=jax.ShapeDtypeStruct((B, H * V_HEAD, S), BF16),
        compiler_params=_params(("parallel", "parallel", "parallel")),
        name="attn",
    )(q, k, vt)


def _wkv_chain(r, k, v, kk, bv, lw, g, reverse):
    C = CHUNK
    sgn = -1 if reverse else 1
    rc = lax.broadcasted_iota(jnp.int32, (C, C), 0)
    cc = lax.broadcasted_iota(jnp.int32, (C, C), 1)
    tri = jnp.where((rc - cc) * sgn >= 0, 1.0, 0.0).astype(BF16)
    lw_hi, lw_lo = _split(lw)
    cum = _mm(tri, lw_hi) + _mm(tri, lw_lo)
    total = jnp.sum(lw, axis=0, keepdims=True)
    e_neg = jnp.exp(-cum)
    e_end = jnp.exp(total - cum)

    row = lax.broadcasted_iota(jnp.int32, (C, GW), 0)
    lane = lax.broadcasted_iota(jnp.int32, (C, GW), 1)
    lane_head = lane // RWKV_HEAD
    ahead = (row - lane % RWKV_HEAD) * sgn
    strict = ahead > 0
    incl = ahead >= 0

    def diag(x):
        xb = x.astype(BF16)
        return jnp.concatenate([jnp.where(lane_head == h, xb, jnp.zeros_like(xb)) for h in range(GROUP)], axis=0)

    def undiag(x):
        return sum(jnp.where(lane_head == h, x[h * C:(h + 1) * C], 0.0) for h in range(GROUP))

    a_t = (-kk * jnp.exp(cum - lw)).astype(BF16)
    r_t = r * jnp.exp(cum)
    dg_v = diag(v)

    yield
    pm = _mm_nt(jnp.concatenate([a_t, r_t.astype(BF16)], axis=0),
                jnp.concatenate([diag(bv * e_neg), diag(k * e_neg)], axis=0))
    n_pow = jnp.where(strict, pm[0:C, 0:GW], 0.0)
    m_ak = jnp.where(strict, pm[0:C, GW:2 * GW], 0.0)
    m_rb = jnp.where(incl, pm[C:2 * C, 0:GW], 0.0).astype(BF16)
    m_rk = jnp.where(incl, pm[C:2 * C, GW:2 * GW], 0.0)
    yield
    wo = _mm(jnp.concatenate([m_ak, m_rk], axis=0).astype(BF16), dg_v)
    w1, o0 = wo[0:C], wo[C:2 * C]

    levels = int(math.log2(C))
    tinv = jnp.where(ahead == 0, 1.0, n_pow)
    for lvl in range(levels):
        yield
        lhs = [] if lvl == levels - 1 else [n_pow]
        lhs += [] if lvl == 0 else [tinv]
        res = _mm(jnp.concatenate(lhs, axis=0).astype(BF16), diag(n_pow))
        if lvl > 0:
            tinv = tinv + res[(len(lhs) - 1) * C:len(lhs) * C]
        n_pow = res[0:C]

    yield
    au = _mm(tinv.astype(BF16), jnp.concatenate([diag(a_t), diag(w1)], axis=1))
    a2, u0 = au[:, 0:GW], au[:, GW:2 * GW]
    yield
    ro = _mm(m_rb, jnp.concatenate([diag(a2), diag(u0)], axis=1))
    r2 = r_t + ro[:, 0:GW]
    o0 = o0 + ro[:, GW:2 * GW]
    yield
    ends = jnp.concatenate([bv * e_end, k * e_end], axis=0).astype(BF16)
    vals = jnp.concatenate([jnp.concatenate([a2, u0], axis=1),
                            jnp.concatenate([jnp.zeros_like(v), v], axis=1)], axis=0).astype(BF16)
    pp = _mm_tn(ends, vals)
    phi, psi = undiag(pp[:, 0:GW]), undiag(pp[:, GW:2 * GW])
    yield
    res = _mm(jnp.concatenate([r2, phi], axis=0).astype(BF16), diag(g))
    on_diag = jnp.where(ahead == 0, jnp.exp(total), 0.0)
    p_end = sum(jnp.where(lane_head == h, jnp.sum(jnp.where(lane_head == h, on_diag, 0.0), axis=1, keepdims=True), 0.0)
                for h in range(GROUP))
    yield res[0:C] + o0, p_end * g + res[C:2 * C] + psi


def _wkv_kernel(fwd_ref, bwd_ref, of_ref, ob_ref, g_ref):
    @pl.when(pl.program_id(1) == 0)
    def _():
        g_ref[...] = jnp.zeros_like(g_ref)

    chains = []
    for bi in range(of_ref.shape[0]):
        for d, (in_ref, o_ref) in enumerate(((fwd_ref, of_ref), (bwd_ref, ob_ref))):
            for gidx in range(RWKV_WIDTH // GW):
                sl = slice(gidx * GW, (gidx + 1) * GW)
                ins = [in_ref[bi, :, seg.start + sl.start:seg.start + sl.stop]
                       for seg in (SCAN_R, SCAN_K, SCAN_V, SCAN_KK, SCAN_BV, SCAN_LW[d])]
                gen = _wkv_chain(*ins, g_ref[bi, d, gidx], reverse=d == 1)
                chains.append((gen, o_ref, bi, sl, d, gidx))
    half = len(chains) // 2
    for wave in (chains[:half], chains[half:]):
        while wave:
            for chain in list(wave):
                gen, o_ref, bi, sl, d, gidx = chain
                out = next(gen)
                if out is not None:
                    o_ref[bi, :, sl] = out[0].astype(o_ref.dtype)
                    g_ref[bi, d, gidx] = out[1]
                    wave.remove(chain)


def _wkv(scan, B, S):
    T = scan.shape[0]
    W = RWKV_WIDTH
    C = CHUNK
    nb = NB_WKV
    nc = S // C
    blk = lambda w, rev: pl.BlockSpec((nb, C, w), lambda b, c: (b, nc - 1 - c if rev else c, 0))
    out = jax.ShapeDtypeStruct((B, S, W), BF16)
    scan = scan.reshape(B, S, SCAN_W)
    o_f, o_b = pl.pallas_call(
        _wkv_kernel,
        grid=(B // nb, nc),
        in_specs=[blk(SCAN_W, False), blk(SCAN_W, True)],
        out_specs=[blk(W, False), blk(W, True)],
        out_shape=[out, out],
        scratch_shapes=[pltpu.VMEM((nb, 2, W // GW, RWKV_HEAD, GW), F32)],
        compiler_params=_params(("parallel", "arbitrary")),
        name="wkv",
    )(scan, scan)
    return o_f.reshape(T, W), o_b.reshape(T, W)


def _mix_kernel(of_ref, ob_ref, post_ref, oat_ref, gate_ref, x_ref, bo_ref, lng_ref, lnb_ref,
                woa_ref, wob_ref, wm_ref, out_ref):
    bo = bo_ref[...]
    bonus = post_ref[:, POST_BONUS].astype(F32)
    ob = None
    for d, o_ref in enumerate((of_ref, ob_ref)):
        o = o_ref[...].astype(F32)
        mu = _head_sums(o, bo) / RWKV_HEAD
        dl = o - mu
        var = _head_sums(dl * dl, bo) / RWKV_HEAD
        y = dl * lax.rsqrt(var + GN_EPS) * lng_ref[...] + lnb_ref[...] + bonus
        y = y * post_ref[:, POST_GATE[d]]
        ob = y if ob is None else ob + y
    y_b = _mm(ob.astype(BF16), wob_ref[...])
    y_a = _mm_tn(oat_ref[0], woa_ref[...])
    mixed = gate_ref[:, 0:D_MODEL] * y_a + gate_ref[:, D_MODEL:2 * D_MODEL] * y_b
    out_ref[...] = x_ref[...] + _mm(mixed.astype(BF16), wm_ref[...])


def _mix(o_f, o_b, post, oat, gates, x2, bo, lng, lnb, woa, wob, wm, B, S):
    T = x2.shape[0]
    tm = TM_MIX
    nj = S // tm
    consts = [bo, lng, lnb, woa, wob, wm]
    tok = lambda w: pl.BlockSpec((tm, w), lambda b, j: (b * nj + j, 0))
    return pl.pallas_call(
        _mix_kernel,
        grid=(B, nj),
        in_specs=[tok(RWKV_WIDTH), tok(RWKV_WIDTH), tok(POST_W),
                  pl.BlockSpec((1, MLA_HEADS * V_HEAD, tm), lambda b, j: (b, 0, j)),
                  pl.BlockSpec((tm, 2 * D_MODEL), lambda b, j: (b * nj + j, 0)),
                  pl.BlockSpec((tm, D_MODEL), lambda b, j: (b * nj + j, 0))]
        + [_const_spec(c.shape) for c in consts],
        out_specs=pl.BlockSpec((tm, D_MODEL), lambda b, j: (b * nj + j, 0)),
        out_shape=jax.ShapeDtypeStruct((T, D_MODEL), F32),
        compiler_params=_params(("parallel", "parallel")),
        name="mix",
    )(o_f, o_b, post, oat, gates, x2, *consts)


def _ffn_kernel(x_ref, before_ref, after_ref, g_ref, wg_ref, wu_ref, cw_ref, cb_ref, wd_ref, out_ref, *, tiles_per_seq):
    tm = x_ref.shape[0]
    x = x_ref[...]
    h_all = _rms(_with_neighbours(x, before_ref, after_ref, tiles_per_seq), g_ref[...]).astype(BF16)
    h = h_all[0:tm]
    chunks = [slice(c, c + TH_FFN) for c in range(0, FFN_HIDDEN, TH_FFN)]
    pres = [_mm(h_all, wg_ref[:, sl]) for sl in chunks]
    ups = [_mm(h, wu_ref[:, sl]) for sl in chunks]
    acc = x
    for sl, pre_all, up in zip(chunks, pres, ups):
        pre = pre_all[0:tm]
        prev, nxt = _shifted(pre, pre_all[tm:tm + 1, :], pre_all[tm + SUBLANE:tm + SUBLANE + 1, :])
        gp = prev * cw_ref[0:1, sl] + pre * cw_ref[1:2, sl] + nxt * cw_ref[2:3, sl] + cb_ref[:, sl]
        act = gp * _sigmoid(gp) * up
        acc = acc + _mm(act.astype(BF16), wd_ref[sl, :])
    out_ref[...] = acc


def _ffn(x1, g, wg, wu, cw, cb, wd, S):
    T = x1.shape[0]
    tm = TM_FFN
    consts = [g, wg, wu, cw, cb, wd]
    return pl.pallas_call(
        functools.partial(_ffn_kernel, tiles_per_seq=S // tm),
        grid=(T // tm,),
        in_specs=[pl.BlockSpec((tm, D_MODEL), lambda i: (i, 0))] + _halo_specs(tm, D_MODEL, T)
        + [_const_spec(c.shape) for c in consts],
        out_specs=pl.BlockSpec((tm, D_MODEL), lambda i: (i, 0)),
        out_shape=jax.ShapeDtypeStruct((T, D_MODEL), F32),
        compiler_params=_params(("parallel",)),
        name="ffn",
    )(x1, x1, x1, *consts)


def _pad_cols(w, width):
    return jnp.pad(w, ((0, 0), (0, width - w.shape[1])))


def _pad_rows(w, height):
    return jnp.pad(w, ((0, height - w.shape[0]), (0, 0)))


def _rw_layout(t):
    W = RWKV_WIDTH
    pad = lambda a: jnp.pad(a, [(0, 0)] * (a.ndim - 1) + [(0, LANE - a.shape[-1])])
    o = 3 * W
    return jnp.concatenate([t[..., 0:o], pad(t[..., o:o + 64]), pad(t[..., o + 64:o + 128]),
                            pad(t[..., o + 128:o + 192]), t[..., o + 192:o + 448]], axis=-1)


def kernel(x, positions, norm_mix_g, w_in, b_gate, q_a_norm_g, kv_a_norm_g, w_uq, w_ukv, qn_norm_g, qr_norm_g,
           kn_norm_g, kr_norm_g, shift_mu, w0, w2, a0, a2, g2, k_k, k_a, r_k, ln_x_g, ln_x_b, w_o, w_merge,
           norm_ffn_g, w_ffn_gate, w_ffn_up, ffn_conv_w, ffn_conv_b, w_ffn_down):
    B, S, D = x.shape
    T = B * S
    depth = norm_mix_g.shape[0]
    H = MLA_HEADS
    mla_cols = Q_LORA + KV_LORA + QK_ROPE
    rw_cols = 3 * RWKV_WIDTH + A_LORA + 2 * DECAY_LORA + 2 * GATE_LORA

    half = QK_ROPE // 2
    inv_freq = ROPE_THETA ** (-jnp.arange(0, QK_ROPE, 2, dtype=F32) / QK_ROPE)
    freq = inv_freq.reshape(half, 1)
    fr = jnp.arange(half)
    place = jnp.zeros((3, half, LANE), F32)
    place = place.at[0, fr, QK_NOPE + fr].set(1.0).at[0, fr, QK_NOPE + half + fr].set(1.0)
    place = place.at[1, fr, QK_NOPE + fr].set(-1.0)
    place = place.at[2, fr, QK_NOPE + half + fr].set(1.0)
    place = place.astype(BF16)
    lane = jnp.arange(LANE)
    base = ((lane < QK_NOPE) | (lane >= QK_NOPE + QK_ROPE)).astype(F32).reshape(1, LANE)
    lane2 = jnp.arange(2 * LANE)
    seg_of = 2 * (lane2 // LANE) + (lane2 % LANE >= QK_NOPE)
    seg = (seg_of[:, None] == seg_of[None, :]).astype(BF16)
    invn = jnp.where(lane2 % LANE < QK_NOPE, 1.0 / QK_NOPE, 1.0 / QK_ROPE).astype(F32).reshape(1, 2 * LANE)
    head_of = jnp.arange(GW) // RWKV_HEAD
    bo = (head_of[:, None] == head_of[None, :]).astype(BF16)
    pos_row = positions.reshape(1, T)
    row = lambda v: v.reshape(1, -1).astype(F32)
    scale = math.log2(math.e) / math.sqrt(QK_NOPE + QK_ROPE)

    x2 = x.reshape(T, D)
    for l in range(depth):
        wi = w_in[l]
        w_mla = jnp.concatenate([wi[:, 0:Q_LORA + KV_LORA], jnp.zeros((D, QK_NOPE), F32),
                                 wi[:, Q_LORA + KV_LORA:mla_cols], jnp.zeros((D, LANE - QK_NOPE - QK_ROPE), F32)], axis=1)
        w_rw = _rw_layout(wi[:, mla_cols:mla_cols + rw_cols])
        w_gate = wi[:, mla_cols + rw_cols:]
        wq = _pad_cols(w_uq[l].reshape(Q_LORA, H, QK_NOPE + QK_ROPE).transpose(1, 0, 2).reshape(H * Q_LORA, -1), LANE)
        wq = wq.reshape(H // 2, 2, Q_LORA, LANE).transpose(0, 2, 1, 3).reshape(H // 2, Q_LORA, 2 * LANE).astype(BF16)
        wkv = w_ukv[l].reshape(KV_LORA, H, QK_NOPE + V_HEAD)
        wk = _pad_cols(wkv[:, :, :QK_NOPE].transpose(1, 0, 2).reshape(H * KV_LORA, QK_NOPE), LANE)
        wk = wk.reshape(H // 2, 2, KV_LORA, LANE).transpose(0, 2, 1, 3).reshape(H // 2, KV_LORA, 2 * LANE).astype(BF16)
        wvt = jnp.pad(wkv[:, :, QK_NOPE:], ((0, 0), (0, 0), (0, V_ROWS - V_HEAD))).reshape(KV_LORA, H * V_ROWS).T.astype(BF16)
        one = jnp.tile((jnp.arange(V_ROWS) == V_HEAD).astype(F32), H).reshape(H * V_ROWS, 1)
        gqh = jnp.tile(_pad_cols(jnp.concatenate([qn_norm_g[l], qr_norm_g[l]]).reshape(1, -1), LANE) * scale, (1, 2))
        gkn = jnp.tile(_pad_cols(kn_norm_g[l].reshape(1, -1), LANE), (1, 2))
        gkr = _pad_cols(jnp.concatenate([jnp.zeros((QK_NOPE,), F32), kr_norm_g[l]]).reshape(1, -1), LANE)
        mu = _rw_layout(shift_mu[l])
        mu = jnp.concatenate([mu, 1.0 - mu[0:1] - mu[1:2]], axis=0)
        a2p = _pad_rows(a2[l], LANE).astype(BF16)
        w2p = jnp.pad(w2[l], ((0, 0), (0, LANE - DECAY_LORA), (0, 0))).astype(BF16)
        consts = [row(norm_mix_g[l]), w_mla.astype(BF16), w_rw.astype(BF16), w_gate.astype(BF16),
                  b_gate[l].reshape(1, 2 * D),
                  row(q_a_norm_g[l]), row(kv_a_norm_g[l]), wq, wk, wvt, gqh, gkn, gkr, freq, place, base, one, seg, invn,
                  mu, row(a0[l]), a2p, w0[l], w2p, g2[l].astype(BF16), row(k_k[l]), row(k_a[l]), row(r_k[l]), bo]

        gates, q, k, vt, scan, post = _front(x2, pos_row, consts, B, S)
        oat = _attn(q, k, vt)
        o_f, o_b = _wkv(scan, B, S)
        wo = w_o[l].astype(BF16)
        x2 = _mix(o_f, o_b, post, oat, gates, x2, bo, row(ln_x_g[l]), row(ln_x_b[l]), wo[:H * V_HEAD],
                  wo[H * V_HEAD:], w_merge[l].astype(BF16), B, S)
        x2 = _ffn(x2, row(norm_ffn_g[l]), w_ffn_gate[l].astype(BF16), w_ffn_up[l].astype(BF16), ffn_conv_w[l],
                  ffn_conv_b[l].reshape(1, -1), w_ffn_down[l].astype(BF16), S)
    return x2.reshape(B, S, D)
```

```python
import functools
import math

import jax
import jax.numpy as jnp
from jax import lax
from jax.experimental import pallas as pl
from jax.experimental.pallas import tpu as pltpu

F32 = jnp.float32
BF16 = jnp.bfloat16

D_MODEL = 1024
MLA_HEADS = 8
QK_NOPE = 64
QK_ROPE = 32
V_HEAD = 64
Q_LORA = 256
KV_LORA = 128
ROPE_THETA = 10000.0
RWKV_HEADS = 8
RWKV_HEAD = 64
RWKV_WIDTH = RWKV_HEADS * RWKV_HEAD
DECAY_LORA = 64
A_LORA = 64
GATE_LORA = 128
GN_EPS = 64e-5
FFN_HIDDEN = 2816
NORM_EPS = 1e-6

LANE = 128
SUBLANE = 8
MXU_COLS = 256
VMEM_LIMIT = 56 * 1024 * 1024

V_ROWS = V_HEAD + 16
MLA_W = 512
RW_W = 3 * RWKV_WIDTH + 5 * LANE

CHUNK = 64
GROUP = 4
GW = GROUP * RWKV_HEAD
assert CHUNK == RWKV_HEAD

_seg = lambda i: slice(i * RWKV_WIDTH, (i + 1) * RWKV_WIDTH)
SCAN_R, SCAN_K, SCAN_V, SCAN_KK, SCAN_BV = (_seg(i) for i in range(5))
SCAN_LW = (_seg(5), _seg(6))
SCAN_W = 7 * RWKV_WIDTH
POST_BONUS = _seg(0)
POST_GATE = (_seg(1), _seg(2))
POST_W = 3 * RWKV_WIDTH

TM_FRONT = 512
GATE_CHUNK = MXU_COLS
TQ_ATTN = 2048
TK_ATTN = 256
QK_SPAN = 2048
ATTN_AHEAD = 1
NB_WKV = 4
TM_MIX = 1024
TM_FFN = 512
TH_FFN = FFN_HIDDEN


def _mm(a, b):
    return jnp.dot(a, b, preferred_element_type=F32)


def _mm_nt(a, b):
    return lax.dot_general(a, b, (((1,), (1,)), ((), ())), preferred_element_type=F32)


def _mm_tn(a, b):
    return lax.dot_general(a, b, (((0,), (0,)), ((), ())), preferred_element_type=F32)


def _split(x):
    hi = x.astype(BF16)
    lo = (x - hi.astype(F32)).astype(BF16)
    return hi, lo


def _head_sums(x, bo):
    xb = x.astype(BF16)
    return jnp.concatenate([_mm(xb[:, c:c + GW], bo) for c in range(0, x.shape[1], GW)], axis=1)


def _sigmoid(x):
    return 0.5 * jnp.tanh(0.5 * x) + 0.5


def _rms(x, g, eps=NORM_EPS):
    return x * lax.rsqrt(jnp.mean(x * x, axis=-1, keepdims=True) + eps) * g


def _const_spec(shape):
    nd = len(shape)
    return pl.BlockSpec(shape, lambda *_: (0,) * nd, pipeline_mode=pl.Buffered(1))


def _params(sem):
    return pltpu.CompilerParams(dimension_semantics=sem, vmem_limit_bytes=VMEM_LIMIT)


def _run_in_turn(streams):
    streams = list(streams)
    while streams:
        for st in list(streams):
            if next(st, "done") == "done":
                streams.remove(st)


def _shifted(p, before, after):
    tm = p.shape[0]
    row = lax.broadcasted_iota(jnp.int32, p.shape, 0)
    prev = jnp.where(row == 0, before, pltpu.roll(p, 1, axis=0))
    nxt = jnp.where(row == tm - 1, after, pltpu.roll(p, tm - 1, axis=0))
    return prev, nxt


def _halo_specs(tm, width, T):
    per = tm // SUBLANE
    before = pl.BlockSpec((SUBLANE, width), lambda i: (jnp.maximum(i * per - 1, 0), 0))
    after = pl.BlockSpec((SUBLANE, width), lambda i: (jnp.minimum((i + 1) * per, T // SUBLANE - 1), 0))
    return [before, after]


def _with_neighbours(x, before_ref, after_ref, tiles_per_seq):
    j = pl.program_id(0) % tiles_per_seq
    before = jnp.where(j == 0, 0.0, before_ref[SUBLANE - 1:SUBLANE, :])
    after = jnp.where(j == tiles_per_seq - 1, 0.0, after_ref[0:1, :])
    width = x.shape[1]
    return jnp.concatenate([x, jnp.broadcast_to(before, (SUBLANE, width)), jnp.broadcast_to(after, (SUBLANE, width))],
                           axis=0)


def _rotary(t, cos, sin_up, sin_dn):
    up = pltpu.roll(t, LANE - QK_ROPE // 2, axis=1)
    dn = pltpu.roll(t, QK_ROPE // 2, axis=1)
    return t * cos + up * sin_up + dn * sin_dn


def _front_kernel(x_ref, xb_ref, xa_ref, pos_ref, gmix_ref, wm_ref, wr_ref, wg_ref, bg_ref,
                  gq_ref, gkv_ref, wq_ref, wk_ref, wvt_ref, gqh_ref, gkn_ref, gkr_ref, freq_ref, place_ref, base_ref,
                  one_ref, seg_ref, invn_ref,
                  mu_ref, a0_ref, a2_ref, w0_ref, w2_ref, g2_ref, kk_ref, ka_ref, rk_ref, bo_ref,
                  gate_out, q_out, k_out, vt_out,
                  scan_out, post_out, *, tiles_per_seq):
    tm = x_ref.shape[0]
    gmix = gmix_ref[...]
    h_all = _rms(_with_neighbours(x_ref[...], xb_ref, xa_ref, tiles_per_seq), gmix).astype(BF16)
    h = h_all[0:tm]

    def gates():
        for c in range(0, 2 * D_MODEL, GATE_CHUNK):
            sl = slice(c, c + GATE_CHUNK)
            gate_out[:, sl] = _sigmoid(_mm(h, wg_ref[:, sl]) + bg_ref[:, sl]).astype(gate_out.dtype)
            yield

    def rwkv():
        p_all = _mm(h_all, wr_ref[...])
        yield
        p = p_all[0:tm]
        prev, nxt = _shifted(p, p_all[tm:tm + 1, :], p_all[tm + SUBLANE:tm + SUBLANE + 1, :])
        u = p * mu_ref[2:3, :] + prev * mu_ref[0:1, :] + nxt * mu_ref[1:2, :]
        W = RWKV_WIDTH
        r, k, v = u[:, 0:W], u[:, W:2 * W], u[:, 2 * W:3 * W]
        scan_out[:, SCAN_R] = r
        scan_out[:, SCAN_V] = v
        yield
        a = _sigmoid(a0_ref[...] + _mm(u[:, 3 * W:3 * W + LANE].astype(BF16), a2_ref[...]))
        bo = bo_ref[...]
        kkf = k * kk_ref[...]
        ss = _head_sums(kkf * kkf, bo)
        yield
        kkn = kkf * jnp.where(ss >= 1e-24, lax.rsqrt(ss), 1e12)
        k = k * (1.0 + (a - 1.0) * ka_ref[...])
        scan_out[:, SCAN_K] = k
        scan_out[:, SCAN_KK] = kkn
        scan_out[:, SCAN_BV] = kkn * a
        yield
        post_out[:, POST_BONUS] = (_head_sums(r * k * rk_ref[...], bo) * v).astype(post_out.dtype)
        yield
        for d in range(2):
            dlo = u[:, 3 * W + (1 + d) * LANE:3 * W + (2 + d) * LANE]
            glo = u[:, 3 * W + (3 + d) * LANE:3 * W + (4 + d) * LANE]
            z = -(w0_ref[d:d + 1, :] + _mm(jnp.tanh(dlo).astype(BF16), w2_ref[d]))
            softplus = jnp.maximum(z, 0.0) + jnp.log(1.0 + jnp.exp(-jnp.abs(z)))
            scan_out[:, SCAN_LW[d]] = -jnp.exp(-softplus - 0.5)
            yield
            post_out[:, POST_GATE[d]] = _mm(_sigmoid(glo).astype(BF16), g2_ref[d]).astype(post_out.dtype)
            yield

    def mla():
        lat = _mm(h, wm_ref[...])
        ang = freq_ref[...] * pos_ref[...].astype(F32)
        place = place_ref[...]
        cos_hl = jnp.concatenate(_split(jnp.cos(ang)), axis=0)
        sin_hl = jnp.concatenate(_split(jnp.sin(ang)), axis=0)
        yield
        cos = _mm_tn(cos_hl, jnp.concatenate([place[0], place[0]], axis=0)) + base_ref[...]
        sin_up = _mm_tn(sin_hl, jnp.concatenate([place[1], place[1]], axis=0))
        sin_dn = _mm_tn(sin_hl, jnp.concatenate([place[2], place[2]], axis=0))
        seg, invn = seg_ref[...], invn_ref[...]

        def head_norm(t, gain):
            ss = _mm((t * t).astype(BF16), seg[0:t.shape[1], 0:t.shape[1]])
            return t * lax.rsqrt(ss * invn[:, 0:t.shape[1]] + NORM_EPS) * gain

        cq = _rms(lat[:, 0:Q_LORA], gq_ref[...]).astype(BF16)
        ckv = _rms(lat[:, Q_LORA:Q_LORA + KV_LORA], gkv_ref[...]).astype(BF16)
        yield
        kr = _rotary(head_norm(lat[:, Q_LORA + KV_LORA:MLA_W], gkr_ref[...]), cos, sin_up, sin_dn)
        vt_out[0] = (_mm_nt(wvt_ref[...], ckv) + one_ref[...]).astype(BF16)
        yield
        for pair in range(MLA_HEADS // 2):
            q2 = head_norm(_mm(cq, wq_ref[pair]), gqh_ref[...])
            for i in range(2):
                q_out[0, 2 * pair + i] = _rotary(q2[:, i * LANE:(i + 1) * LANE], cos, sin_up, sin_dn).astype(BF16)
            yield
            k2 = head_norm(_mm(ckv, wk_ref[pair]), gkn_ref[...])
            for i in range(2):
                k_out[0, 2 * pair + i] = (k2[:, i * LANE:(i + 1) * LANE] + kr).astype(BF16)
            yield

    _run_in_turn([gates(), rwkv(), mla()])


def _front(x2, pos_row, consts, B, S):
    T = x2.shape[0]
    tm = TM_FRONT
    nj = S // tm
    H = MLA_HEADS
    tok = lambda w: pl.BlockSpec((tm, w), lambda i: (i, 0))
    heads = pl.BlockSpec((1, H, tm, LANE), lambda i: (i // nj, 0, i % nj, 0))
    return pl.pallas_call(
        functools.partial(_front_kernel, tiles_per_seq=nj),
        grid=(T // tm,),
        in_specs=[tok(D_MODEL)] + _halo_specs(tm, D_MODEL, T) + [pl.BlockSpec((1, tm), lambda i: (0, i))]
        + [_const_spec(c.shape) for c in consts],
        out_specs=[tok(2 * D_MODEL), heads, heads, pl.BlockSpec((1, H * V_ROWS, tm), lambda i: (i // nj, 0, i % nj))]
        + [tok(SCAN_W), tok(POST_W)],
        out_shape=[jax.ShapeDtypeStruct((T, 2 * D_MODEL), BF16), jax.ShapeDtypeStruct((B, H, S, LANE), BF16),
                   jax.ShapeDtypeStruct((B, H, S, LANE), BF16), jax.ShapeDtypeStruct((B, H * V_ROWS, S), BF16),
                   jax.ShapeDtypeStruct((T, SCAN_W), F32), jax.ShapeDtypeStruct((T, POST_W), BF16)],
        compiler_params=_params(("parallel",)),
        name="front",
    )(x2, x2, x2, pos_row, *consts)


def _attn_kernel(q_ref, k_ref, vt_ref, ot_ref):
    S = k_ref.shape[2]
    tk = TK_ATTN
    subs = [q_ref[0, 0, i:i + MXU_COLS, :] for i in range(0, q_ref.shape[2], MXU_COLS)]
    per_span = QK_SPAN // tk
    nspan = S // QK_SPAN

    def scores_for(g):
        kb = k_ref[0, 0, g * QK_SPAN:(g + 1) * QK_SPAN, :]
        return [_mm_nt(kb, q) for q in subs]

    m = [jnp.full((1, MXU_COLS), -jnp.inf, F32) for _ in subs]
    acc = [jnp.zeros((V_ROWS, MXU_COLS), F32) for _ in subs]
    scores = [scores_for(g) for g in range(min(ATTN_AHEAD, nspan))]
    for g in range(nspan):
        if g + ATTN_AHEAD < nspan:
            scores.append(scores_for(g + ATTN_AHEAD))
        for jj in range(per_span):
            j = g * per_span + jj
            vb = vt_ref[0, :, j * tk:(j + 1) * tk]
            for i, s_span in enumerate(scores[g]):
                s = s_span[jj * tk:(jj + 1) * tk]
                m_new = jnp.maximum(m[i], jnp.max(s, axis=0, keepdims=True))
                p = jnp.exp2(s - m_new)
                alpha = jnp.exp2(m[i] - m_new)
                acc[i] = alpha * acc[i] + _mm(vb, p.astype(BF16))
                m[i] = m_new
        scores[g] = None
    for i in range(len(subs)):
        ot_ref[0, :, i * MXU_COLS:(i + 1) * MXU_COLS] = (acc[i][0:V_HEAD] / acc[i][V_HEAD:V_HEAD + 1]).astype(BF16)


def _attn(q, k, vt):
    B, H, S, _ = q.shape
    tq = TQ_ATTN
    return pl.pallas_call(
        _attn_kernel,
        grid=(B, H, S // tq),
        in_specs=[pl.BlockSpec((1, 1, tq, LANE), lambda b, h, i: (b, h, i, 0)),
                  pl.BlockSpec((1, 1, S, LANE), lambda b, h, i: (b, h, 0, 0)),
                  pl.BlockSpec((1, V_ROWS, S), lambda b, h, i: (b, h, 0))],
        out_specs=pl.BlockSpec((1, V_HEAD, tq), lambda b, h, i: (b, h, i)),
        out_shape=jax.ShapeDtypeStruct((B, H * V_HEAD, S), BF16),
        compiler_params=_params(("parallel", "parallel", "parallel")),
        name="attn",
    )(q, k, vt)


def _wkv_chain(r, k, v, kk, bv, lw, g, reverse):
    C = CHUNK
    sgn = -1 if reverse else 1
    rc = lax.broadcasted_iota(jnp.int32, (C, C), 0)
    cc = lax.broadcasted_iota(jnp.int32, (C, C), 1)
    tri = jnp.where((rc - cc) * sgn >= 0, 1.0, 0.0).astype(BF16)
    lw_hi, lw_lo = _split(lw)
    cum = _mm(tri, lw_hi) + _mm(tri, lw_lo)
    total = jnp.sum(lw, axis=0, keepdims=True)
    e_neg = jnp.exp(-cum)
    e_end = jnp.exp(total - cum)

    row = lax.broadcasted_iota(jnp.int32, (C, GW), 0)
    lane = lax.broadcasted_iota(jnp.int32, (C, GW), 1)
    lane_head = lane // RWKV_HEAD
    ahead = (row - lane % RWKV_HEAD) * sgn
    strict = ahead > 0
    incl = ahead >= 0

    def diag(x):
        xb = x.astype(BF16)
        return jnp.concatenate([jnp.where(lane_head == h, xb, jnp.zeros_like(xb)) for h in range(GROUP)], axis=0)

    def undiag(x):
        return sum(jnp.where(lane_head == h, x[h * C:(h + 1) * C], 0.0) for h in range(GROUP))

    a_t = (-kk * jnp.exp(cum - lw)).astype(BF16)
    r_t = r * jnp.exp(cum)
    dg_v = diag(v)

    yield
    pm = _mm_nt(jnp.concatenate([a_t, r_t.astype(BF16)], axis=0),
                jnp.concatenate([diag(bv * e_neg), diag(k * e_neg)], axis=0))
    n_pow = jnp.where(strict, pm[0:C, 0:GW], 0.0)
    m_ak = jnp.where(strict, pm[0:C, GW:2 * GW], 0.0)
    m_rb = jnp.where(incl, pm[C:2 * C, 0:GW], 0.0).astype(BF16)
    m_rk = jnp.where(incl, pm[C:2 * C, GW:2 * GW], 0.0)
    yield
    wo = _mm(jnp.concatenate([m_ak, m_rk], axis=0).astype(BF16), dg_v)
    w1, o0 = wo[0:C], wo[C:2 * C]

    levels = int(math.log2(C))
    tinv = jnp.where(ahead == 0, 1.0, n_pow)
    for lvl in range(levels):
        yield
        lhs = [] if lvl == levels - 1 else [n_pow]
        lhs += [] if lvl == 0 else [tinv]
        res = _mm(jnp.concatenate(lhs, axis=0).astype(BF16), diag(n_pow))
        if lvl > 0:
            tinv = tinv + res[(len(lhs) - 1) * C:len(lhs) * C]
        n_pow = res[0:C]

    yield
    au = _mm(tinv.astype(BF16), jnp.concatenate([diag(a_t), diag(w1)], axis=1))
    a2, u0 = au[:, 0:GW], au[:, GW:2 * GW]
    yield
    ro = _mm(m_rb, jnp.concatenate([diag(a2), diag(u0)], axis=1))
    r2 = r_t + ro[:, 0:GW]
    o0 = o0 + ro[:, GW:2 * GW]
    yield
    ends = jnp.concatenate([bv * e_end, k * e_end], axis=0).astype(BF16)
    vals = jnp.concatenate([jnp.concatenate([a2, u0], axis=1),
                            jnp.concatenate([jnp.zeros_like(v), v], axis=1)], axis=0).astype(BF16)
    pp = _mm_tn(ends, vals)
    phi, psi = undiag(pp[:, 0:GW]), undiag(pp[:, GW:2 * GW])
    yield
    res = _mm(jnp.concatenate([r2, phi], axis=0).astype(BF16), diag(g))
    on_diag = jnp.where(ahead == 0, jnp.exp(total), 0.0)
    p_end = sum(jnp.where(lane_head == h, jnp.sum(jnp.where(lane_head == h, on_diag, 0.0), axis=1, keepdims=True), 0.0)
                for h in range(GROUP))
    yield res[0:C] + o0, p_end * g + res[C:2 * C] + psi


def _wkv_kernel(fwd_ref, bwd_ref, of_ref, ob_ref, g_ref):
    @pl.when(pl.program_id(1) == 0)
    def _():
        g_ref[...] = jnp.zeros_like(g_ref)

    chains = []
    for bi in range(of_ref.shape[0]):
        for d, (in_ref, o_ref) in enumerate(((fwd_ref, of_ref), (bwd_ref, ob_ref))):
            for gidx in range(RWKV_WIDTH // GW):
                sl = slice(gidx * GW, (gidx + 1) * GW)
                ins = [in_ref[bi, :, seg.start + sl.start:seg.start + sl.stop]
                       for seg in (SCAN_R, SCAN_K, SCAN_V, SCAN_KK, SCAN_BV, SCAN_LW[d])]
                gen = _wkv_chain(*ins, g_ref[bi, d, gidx], reverse=d == 1)
                chains.append((gen, o_ref, bi, sl, d, gidx))
    while chains:
        for chain in list(chains):
            gen, o_ref, bi, sl, d, gidx = chain
            out = next(gen)
            if out is not None:
                o_ref[bi, :, sl] = out[0].astype(o_ref.dtype)
                g_ref[bi, d, gidx] = out[1]
                chains.remove(chain)


def _wkv(scan, B, S):
    T = scan.shape[0]
    W = RWKV_WIDTH
    C = CHUNK
    nb = NB_WKV
    nc = S // C
    blk = lambda w, rev: pl.BlockSpec((nb, C, w), lambda b, c: (b, nc - 1 - c if rev else c, 0))
    out = jax.ShapeDtypeStruct((B, S, W), BF16)
    scan = scan.reshape(B, S, SCAN_W)
    o_f, o_b = pl.pallas_call(
        _wkv_kernel,
        grid=(B // nb, nc),
        in_specs=[blk(SCAN_W, False), blk(SCAN_W, True)],
        out_specs=[blk(W, False), blk(W, True)],
        out_shape=[out, out],
        scratch_shapes=[pltpu.VMEM((nb, 2, W // GW, RWKV_HEAD, GW), F32)],
        compiler_params=_params(("parallel", "arbitrary")),
        name="wkv",
    )(scan, scan)
    return o_f.reshape(T, W), o_b.reshape(T, W)


def _mix_kernel(of_ref, ob_ref, post_ref, oat_ref, gate_ref, x_ref, bo_ref, lng_ref, lnb_ref,
                woa_ref, wob_ref, wm_ref, out_ref):
    bo = bo_ref[...]
    bonus = post_ref[:, POST_BONUS].astype(F32)
    ob = None
    for d, o_ref in enumerate((of_ref, ob_ref)):
        o = o_ref[...].astype(F32)
        mu = _head_sums(o, bo) / RWKV_HEAD
        dl = o - mu
        var = _head_sums(dl * dl, bo) / RWKV_HEAD
        y = dl * lax.rsqrt(var + GN_EPS) * lng_ref[...] + lnb_ref[...] + bonus
        y = y * post_ref[:, POST_GATE[d]]
        ob = y if ob is None else ob + y
    y_b = _mm(ob.astype(BF16), wob_ref[...])
    y_a = _mm_tn(oat_ref[0], woa_ref[...])
    mixed = gate_ref[:, 0:D_MODEL] * y_a + gate_ref[:, D_MODEL:2 * D_MODEL] * y_b
    out_ref[...] = x_ref[...] + _mm(mixed.astype(BF16), wm_ref[...])


def _mix(o_f, o_b, post, oat, gates, x2, bo, lng, lnb, woa, wob, wm, B, S):
    T = x2.shape[0]
    tm = TM_MIX
    nj = S // tm
    consts = [bo, lng, lnb, woa, wob, wm]
    tok = lambda w: pl.BlockSpec((tm, w), lambda b, j: (b * nj + j, 0))
    return pl.pallas_call(
        _mix_kernel,
        grid=(B, nj),
        in_specs=[tok(RWKV_WIDTH), tok(RWKV_WIDTH), tok(POST_W),
                  pl.BlockSpec((1, MLA_HEADS * V_HEAD, tm), lambda b, j: (b, 0, j)),
                  pl.BlockSpec((tm, 2 * D_MODEL), lambda b, j: (b * nj + j, 0)),
                  pl.BlockSpec((tm, D_MODEL), lambda b, j: (b * nj + j, 0))]
        + [_const_spec(c.shape) for c in consts],
        out_specs=pl.BlockSpec((tm, D_MODEL), lambda b, j: (b * nj + j, 0)),
        out_shape=jax.ShapeDtypeStruct((T, D_MODEL), F32),
        compiler_params=_params(("parallel", "parallel")),
        name="mix",
    )(o_f, o_b, post, oat, gates, x2, *consts)


def _ffn_kernel(x_ref, before_ref, after_ref, g_ref, wg_ref, wu_ref, cw_ref, cb_ref, wd_ref, out_ref, *, tiles_per_seq):
    tm = x_ref.shape[0]
    x = x_ref[...]
    h_all = _rms(_with_neighbours(x, before_ref, after_ref, tiles_per_seq), g_ref[...]).astype(BF16)
    h = h_all[0:tm]
    chunks = [slice(c, c + TH_FFN) for c in range(0, FFN_HIDDEN, TH_FFN)]
    pres = [_mm(h_all, wg_ref[:, sl]) for sl in chunks]
    ups = [_mm(h, wu_ref[:, sl]) for sl in chunks]
    acc = x
    for sl, pre_all, up in zip(chunks, pres, ups):
        pre = pre_all[0:tm]
        prev, nxt = _shifted(pre, pre_all[tm:tm + 1, :], pre_all[tm + SUBLANE:tm + SUBLANE + 1, :])
        gp = prev * cw_ref[0:1, sl] + pre * cw_ref[1:2, sl] + nxt * cw_ref[2:3, sl] + cb_ref[:, sl]
        act = gp * _sigmoid(gp) * up
        acc = acc + _mm(act.astype(BF16), wd_ref[sl, :])
    out_ref[...] = acc


def _ffn(x1, g, wg, wu, cw, cb, wd, S):
    T = x1.shape[0]
    tm = TM_FFN
    consts = [g, wg, wu, cw, cb, wd]
    return pl.pallas_call(
        functools.partial(_ffn_kernel, tiles_per_seq=S // tm),
        grid=(T // tm,),
        in_specs=[pl.BlockSpec((tm, D_MODEL), lambda i: (i, 0))] + _halo_specs(tm, D_MODEL, T)
        + [_const_spec(c.shape) for c in consts],
        out_specs=pl.BlockSpec((tm, D_MODEL), lambda i: (i, 0)),
        out_shape=jax.ShapeDtypeStruct((T, D_MODEL), F32),
        compiler_params=_params(("parallel",)),
        name="ffn",
    )(x1, x1, x1, *consts)


def _pad_cols(w, width):
    return jnp.pad(w, ((0, 0), (0, width - w.shape[1])))


def _pad_rows(w, height):
    return jnp.pad(w, ((0, height - w.shape[0]), (0, 0)))


def _rw_layout(t):
    W = RWKV_WIDTH
    pad = lambda a: jnp.pad(a, [(0, 0)] * (a.ndim - 1) + [(0, LANE - a.shape[-1])])
    o = 3 * W
    return jnp.concatenate([t[..., 0:o], pad(t[..., o:o + 64]), pad(t[..., o + 64:o + 128]),
                            pad(t[..., o + 128:o + 192]), t[..., o + 192:o + 448]], axis=-1)


def kernel(x, positions, norm_mix_g, w_in, b_gate, q_a_norm_g, kv_a_norm_g, w_uq, w_ukv, qn_norm_g, qr_norm_g,
           kn_norm_g, kr_norm_g, shift_mu, w0, w2, a0, a2, g2, k_k, k_a, r_k, ln_x_g, ln_x_b, w_o, w_merge,
           norm_ffn_g, w_ffn_gate, w_ffn_up, ffn_conv_w, ffn_conv_b, w_ffn_down):
    B, S, D = x.shape
    T = B * S
    depth = norm_mix_g.shape[0]
    H = MLA_HEADS
    mla_cols = Q_LORA + KV_LORA + QK_ROPE
    rw_cols = 3 * RWKV_WIDTH + A_LORA + 2 * DECAY_LORA + 2 * GATE_LORA

    half = QK_ROPE // 2
    inv_freq = ROPE_THETA ** (-jnp.arange(0, QK_ROPE, 2, dtype=F32) / QK_ROPE)
    freq = inv_freq.reshape(half, 1)
    fr = jnp.arange(half)
    place = jnp.zeros((3, half, LANE), F32)
    place = place.at[0, fr, QK_NOPE + fr].set(1.0).at[0, fr, QK_NOPE + half + fr].set(1.0)
    place = place.at[1, fr, QK_NOPE + fr].set(-1.0)
    place = place.at[2, fr, QK_NOPE + half + fr].set(1.0)
    place = place.astype(BF16)
    lane = jnp.arange(LANE)
    base = ((lane < QK_NOPE) | (lane >= QK_NOPE + QK_ROPE)).astype(F32).reshape(1, LANE)
    lane2 = jnp.arange(2 * LANE)
    seg_of = 2 * (lane2 // LANE) + (lane2 % LANE >= QK_NOPE)
    seg = (seg_of[:, None] == seg_of[None, :]).astype(BF16)
    invn = jnp.where(lane2 % LANE < QK_NOPE, 1.0 / QK_NOPE, 1.0 / QK_ROPE).astype(F32).reshape(1, 2 * LANE)
    head_of = jnp.arange(GW) // RWKV_HEAD
    bo = (head_of[:, None] == head_of[None, :]).astype(BF16)
    pos_row = positions.reshape(1, T)
    row = lambda v: v.reshape(1, -1).astype(F32)
    scale = math.log2(math.e) / math.sqrt(QK_NOPE + QK_ROPE)

    x2 = x.reshape(T, D)
    for l in range(depth):
        wi = w_in[l]
        w_mla = jnp.concatenate([wi[:, 0:Q_LORA + KV_LORA], jnp.zeros((D, QK_NOPE), F32),
                                 wi[:, Q_LORA + KV_LORA:mla_cols], jnp.zeros((D, LANE - QK_NOPE - QK_ROPE), F32)], axis=1)
        w_rw = _rw_layout(wi[:, mla_cols:mla_cols + rw_cols])
        w_gate = wi[:, mla_cols + rw_cols:]
        wq = _pad_cols(w_uq[l].reshape(Q_LORA, H, QK_NOPE + QK_ROPE).transpose(1, 0, 2).reshape(H * Q_LORA, -1), LANE)
        wq = wq.reshape(H // 2, 2, Q_LORA, LANE).transpose(0, 2, 1, 3).reshape(H // 2, Q_LORA, 2 * LANE).astype(BF16)
        wkv = w_ukv[l].reshape(KV_LORA, H, QK_NOPE + V_HEAD)
        wk = _pad_cols(wkv[:, :, :QK_NOPE].transpose(1, 0, 2).reshape(H * KV_LORA, QK_NOPE), LANE)
        wk = wk.reshape(H // 2, 2, KV_LORA, LANE).transpose(0, 2, 1, 3).reshape(H // 2, KV_LORA, 2 * LANE).astype(BF16)
        wvt = jnp.pad(wkv[:, :, QK_NOPE:], ((0, 0), (0, 0), (0, V_ROWS - V_HEAD))).reshape(KV_LORA, H * V_ROWS).T.astype(BF16)
        one = jnp.tile((jnp.arange(V_ROWS) == V_HEAD).astype(F32), H).reshape(H * V_ROWS, 1)
        gqh = jnp.tile(_pad_cols(jnp.concatenate([qn_norm_g[l], qr_norm_g[l]]).reshape(1, -1), LANE) * scale, (1, 2))
        gkn = jnp.tile(_pad_cols(kn_norm_g[l].reshape(1, -1), LANE), (1, 2))
        gkr = _pad_cols(jnp.concatenate([jnp.zeros((QK_NOPE,), F32), kr_norm_g[l]]).reshape(1, -1), LANE)
        mu = _rw_layout(shift_mu[l])
        mu = jnp.concatenate([mu, 1.0 - mu[0:1] - mu[1:2]], axis=0)
        a2p = _pad_rows(a2[l], LANE).astype(BF16)
        w2p = jnp.pad(w2[l], ((0, 0), (0, LANE - DECAY_LORA), (0, 0))).astype(BF16)
        consts = [row(norm_mix_g[l]), w_mla.astype(BF16), w_rw.astype(BF16), w_gate.astype(BF16),
                  b_gate[l].reshape(1, 2 * D),
                  row(q_a_norm_g[l]), row(kv_a_norm_g[l]), wq, wk, wvt, gqh, gkn, gkr, freq, place, base, one, seg, invn,
                  mu, row(a0[l]), a2p, w0[l], w2p, g2[l].astype(BF16), row(k_k[l]), row(k_a[l]), row(r_k[l]), bo]

        gates, q, k, vt, scan, post = _front(x2, pos_row, consts, B, S)
        oat = _attn(q, k, vt)
        o_f, o_b = _wkv(scan, B, S)
        wo = w_o[l].astype(BF16)
        x2 = _mix(o_f, o_b, post, oat, gates, x2, bo, row(ln_x_g[l]), row(ln_x_b[l]), wo[:H * V_HEAD],
                  wo[H * V_HEAD:], w_merge[l].astype(BF16), B, S)
        x2 = _ffn(x2, row(norm_ffn_g[l]), w_ffn_gate[l].astype(BF16), w_ffn_up[l].astype(BF16), ffn_conv_w[l],
                  ffn_conv_b[l].reshape(1, -1), w_ffn_down[l].astype(BF16), S)
    return x2.reshape(B, S, D)
```

```python
import functools
import math

import jax
import jax.numpy as jnp
from jax import lax
from jax.experimental import pallas as pl
from jax.experimental.pallas import tpu as pltpu

F32 = jnp.float32
BF16 = jnp.bfloat16

D_MODEL = 1024
MLA_HEADS = 8
QK_NOPE = 64
QK_ROPE = 32
V_HEAD = 64
Q_LORA = 256
KV_LORA = 128
ROPE_THETA = 10000.0
RWKV_HEADS = 8
RWKV_HEAD = 64
RWKV_WIDTH = RWKV_HEADS * RWKV_HEAD
DECAY_LORA = 64
A_LORA = 64
GATE_LORA = 128
GN_EPS = 64e-5
FFN_HIDDEN = 2816
NORM_EPS = 1e-6

LANE = 128
SUBLANE = 8
MXU_COLS = 256
VMEM_LIMIT = 56 * 1024 * 1024

V_ROWS = V_HEAD + 16
MLA_W = 512
RW_W = 3 * RWKV_WIDTH + 5 * LANE

CHUNK = 64
GROUP = 4
GW = GROUP * RWKV_HEAD
assert CHUNK == RWKV_HEAD

_seg = lambda i: slice(i * RWKV_WIDTH, (i + 1) * RWKV_WIDTH)
SCAN_R, SCAN_K, SCAN_V, SCAN_KK, SCAN_BV = (_seg(i) for i in range(5))
SCAN_W = 5 * RWKV_WIDTH
SCAN_LW = (_seg(0), _seg(1))
LW_W = 2 * RWKV_WIDTH
POST_BONUS = _seg(0)
POST_GATE = (_seg(1), _seg(2))
POST_W = 3 * RWKV_WIDTH

TM_FRONT = 512
GATE_CHUNK = MXU_COLS
TQ_ATTN = 2048
TK_ATTN = 256
QK_SPAN = 2048
ATTN_AHEAD = 1
NB_WKV = 4
TM_MIX = 1024
TM_FFN = 512
TH_FFN = FFN_HIDDEN


def _mm(a, b):
    return jnp.dot(a, b, preferred_element_type=F32)


def _mm_nt(a, b):
    return lax.dot_general(a, b, (((1,), (1,)), ((), ())), preferred_element_type=F32)


def _mm_tn(a, b):
    return lax.dot_general(a, b, (((0,), (0,)), ((), ())), preferred_element_type=F32)


def _split(x):
    hi = x.astype(BF16)
    lo = (x - hi.astype(F32)).astype(BF16)
    return hi, lo


def _head_sums(x, bo):
    xb = x.astype(BF16)
    return jnp.concatenate([_mm(xb[:, c:c + GW], bo) for c in range(0, x.shape[1], GW)], axis=1)


def _sigmoid(x):
    return 0.5 * jnp.tanh(0.5 * x) + 0.5


def _rms(x, g, eps=NORM_EPS):
    return x * lax.rsqrt(jnp.mean(x * x, axis=-1, keepdims=True) + eps) * g


def _const_spec(shape):
    nd = len(shape)
    return pl.BlockSpec(shape, lambda *_: (0,) * nd, pipeline_mode=pl.Buffered(1))


def _params(sem):
    return pltpu.CompilerParams(dimension_semantics=sem, vmem_limit_bytes=VMEM_LIMIT)


def _run_in_turn(streams):
    streams = list(streams)
    while streams:
        for st in list(streams):
            if next(st, "done") == "done":
                streams.remove(st)


def _shifted(p, before, after):
    tm = p.shape[0]
    row = lax.broadcasted_iota(jnp.int32, p.shape, 0)
    prev = jnp.where(row == 0, before, pltpu.roll(p, 1, axis=0))
    nxt = jnp.where(row == tm - 1, after, pltpu.roll(p, tm - 1, axis=0))
    return prev, nxt


def _halo_specs(tm, width, T):
    per = tm // SUBLANE
    before = pl.BlockSpec((SUBLANE, width), lambda i: (jnp.maximum(i * per - 1, 0), 0))
    after = pl.BlockSpec((SUBLANE, width), lambda i: (jnp.minimum((i + 1) * per, T // SUBLANE - 1), 0))
    return [before, after]


def _with_neighbours(x, before_ref, after_ref, tiles_per_seq):
    j = pl.program_id(0) % tiles_per_seq
    before = jnp.where(j == 0, 0.0, before_ref[SUBLANE - 1:SUBLANE, :])
    after = jnp.where(j == tiles_per_seq - 1, 0.0, after_ref[0:1, :])
    width = x.shape[1]
    return jnp.concatenate([x, jnp.broadcast_to(before, (SUBLANE, width)), jnp.broadcast_to(after, (SUBLANE, width))],
                           axis=0)


def _rotary(t, cos, sin_up, sin_dn):
    up = pltpu.roll(t, LANE - QK_ROPE // 2, axis=1)
    dn = pltpu.roll(t, QK_ROPE // 2, axis=1)
    return t * cos + up * sin_up + dn * sin_dn


def _front_kernel(x_ref, xb_ref, xa_ref, pos_ref, gmix_ref, wm_ref, wr_ref, wg_ref, bg_ref,
                  gq_ref, gkv_ref, wq_ref, wk_ref, wvt_ref, gqh_ref, gkn_ref, gkr_ref, freq_ref, place_ref, base_ref,
                  one_ref, seg_ref, invn_ref,
                  mu_ref, a0_ref, a2_ref, w0_ref, w2_ref, g2_ref, kk_ref, ka_ref, rk_ref, bo_ref,
                  gate_out, q_out, k_out, vt_out,
                  scan_out, lw_out, post_out, *, tiles_per_seq):
    tm = x_ref.shape[0]
    gmix = gmix_ref[...]
    h_all = _rms(_with_neighbours(x_ref[...], xb_ref, xa_ref, tiles_per_seq), gmix).astype(BF16)
    h = h_all[0:tm]

    def gates():
        for c in range(0, 2 * D_MODEL, GATE_CHUNK):
            sl = slice(c, c + GATE_CHUNK)
            gate_out[:, sl] = _sigmoid(_mm(h, wg_ref[:, sl]) + bg_ref[:, sl]).astype(gate_out.dtype)
            yield

    def rwkv():
        p_all = _mm(h_all, wr_ref[...])
        yield
        p = p_all[0:tm]
        prev, nxt = _shifted(p, p_all[tm:tm + 1, :], p_all[tm + SUBLANE:tm + SUBLANE + 1, :])
        u = p * mu_ref[2:3, :] + prev * mu_ref[0:1, :] + nxt * mu_ref[1:2, :]
        W = RWKV_WIDTH
        r, k, v = u[:, 0:W], u[:, W:2 * W], u[:, 2 * W:3 * W]
        scan_out[:, SCAN_R] = r.astype(scan_out.dtype)
        scan_out[:, SCAN_V] = v.astype(scan_out.dtype)
        yield
        a = _sigmoid(a0_ref[...] + _mm(u[:, 3 * W:3 * W + LANE].astype(BF16), a2_ref[...]))
        bo = bo_ref[...]
        kkf = k * kk_ref[...]
        ss = _head_sums(kkf * kkf, bo)
        yield
        kkn = kkf * jnp.where(ss >= 1e-24, lax.rsqrt(ss), 1e12)
        k = k * (1.0 + (a - 1.0) * ka_ref[...])
        scan_out[:, SCAN_K] = k.astype(scan_out.dtype)
        scan_out[:, SCAN_KK] = kkn.astype(scan_out.dtype)
        scan_out[:, SCAN_BV] = (kkn * a).astype(scan_out.dtype)
        yield
        post_out[:, POST_BONUS] = (_head_sums(r * k * rk_ref[...], bo) * v).astype(post_out.dtype)
        yield
        for d in range(2):
            dlo = u[:, 3 * W + (1 + d) * LANE:3 * W + (2 + d) * LANE]
            glo = u[:, 3 * W + (3 + d) * LANE:3 * W + (4 + d) * LANE]
            z = -(w0_ref[d:d + 1, :] + _mm(jnp.tanh(dlo).astype(BF16), w2_ref[d]))
            softplus = jnp.maximum(z, 0.0) + jnp.log(1.0 + jnp.exp(-jnp.abs(z)))
            lw_out[:, SCAN_LW[d]] = -jnp.exp(-softplus - 0.5)
            yield
            post_out[:, POST_GATE[d]] = _mm(_sigmoid(glo).astype(BF16), g2_ref[d]).astype(post_out.dtype)
            yield

    def mla():
        lat = _mm(h, wm_ref[...])
        ang = freq_ref[...] * pos_ref[...].astype(F32)
        place = place_ref[...]
        cos_hl = jnp.concatenate(_split(jnp.cos(ang)), axis=0)
        sin_hl = jnp.concatenate(_split(jnp.sin(ang)), axis=0)
        yield
        cos = _mm_tn(cos_hl, jnp.concatenate([place[0], place[0]], axis=0)) + base_ref[...]
        sin_up = _mm_tn(sin_hl, jnp.concatenate([place[1], place[1]], axis=0))
        sin_dn = _mm_tn(sin_hl, jnp.concatenate([place[2], place[2]], axis=0))
        seg, invn = seg_ref[...], invn_ref[...]

        def head_norm(t, gain):
            ss = _mm((t * t).astype(BF16), seg[0:t.shape[1], 0:t.shape[1]])
            return t * lax.rsqrt(ss * invn[:, 0:t.shape[1]] + NORM_EPS) * gain

        cq = _rms(lat[:, 0:Q_LORA], gq_ref[...]).astype(BF16)
        ckv = _rms(lat[:, Q_LORA:Q_LORA + KV_LORA], gkv_ref[...]).astype(BF16)
        yield
        kr = _rotary(head_norm(lat[:, Q_LORA + KV_LORA:MLA_W], gkr_ref[...]), cos, sin_up, sin_dn)
        vt_out[0] = (_mm_nt(wvt_ref[...], ckv) + one_ref[...]).astype(BF16)
        yield
        for pair in range(MLA_HEADS // 2):
            q2 = head_norm(_mm(cq, wq_ref[pair]), gqh_ref[...])
            for i in range(2):
                q_out[0, 2 * pair + i] = _rotary(q2[:, i * LANE:(i + 1) * LANE], cos, sin_up, sin_dn).astype(BF16)
            yield
            k2 = head_norm(_mm(ckv, wk_ref[pair]), gkn_ref[...])
            for i in range(2):
                k_out[0, 2 * pair + i] = (k2[:, i * LANE:(i + 1) * LANE] + kr).astype(BF16)
            yield

    _run_in_turn([gates(), rwkv(), mla()])


def _front(x2, pos_row, consts, B, S):
    T = x2.shape[0]
    tm = TM_FRONT
    nj = S // tm
    H = MLA_HEADS
    tok = lambda w: pl.BlockSpec((tm, w), lambda i: (i, 0))
    heads = pl.BlockSpec((1, H, tm, LANE), lambda i: (i // nj, 0, i % nj, 0))
    return pl.pallas_call(
        functools.partial(_front_kernel, tiles_per_seq=nj),
        grid=(T // tm,),
        in_specs=[tok(D_MODEL)] + _halo_specs(tm, D_MODEL, T) + [pl.BlockSpec((1, tm), lambda i: (0, i))]
        + [_const_spec(c.shape) for c in consts],
        out_specs=[tok(2 * D_MODEL), heads, heads, pl.BlockSpec((1, H * V_ROWS, tm), lambda i: (i // nj, 0, i % nj))]
        + [tok(SCAN_W), tok(LW_W), tok(POST_W)],
        out_shape=[jax.ShapeDtypeStruct((T, 2 * D_MODEL), BF16), jax.ShapeDtypeStruct((B, H, S, LANE), BF16),
                   jax.ShapeDtypeStruct((B, H, S, LANE), BF16), jax.ShapeDtypeStruct((B, H * V_ROWS, S), BF16),
                   jax.ShapeDtypeStruct((T, SCAN_W), BF16), jax.ShapeDtypeStruct((T, LW_W), F32),
                   jax.ShapeDtypeStruct((T, POST_W), BF16)],
        compiler_params=_params(("parallel",)),
        name="front",
    )(x2, x2, x2, pos_row, *consts)


def _attn_kernel(q_ref, k_ref, vt_ref, ot_ref):
    S = k_ref.shape[2]
    tk = TK_ATTN
    subs = [q_ref[0, 0, i:i + MXU_COLS, :] for i in range(0, q_ref.shape[2], MXU_COLS)]
    per_span = QK_SPAN // tk
    nspan = S // QK_SPAN

    def scores_for(g):
        kb = k_ref[0, 0, g * QK_SPAN:(g + 1) * QK_SPAN, :]
        return [_mm_nt(kb, q) for q in subs]

    m = [jnp.full((1, MXU_COLS), -jnp.inf, F32) for _ in subs]
    acc = [jnp.zeros((V_ROWS, MXU_COLS), F32) for _ in subs]
    scores = [scores_for(g) for g in range(min(ATTN_AHEAD, nspan))]
    for g in range(nspan):
        if g + ATTN_AHEAD < nspan:
            scores.append(scores_for(g + ATTN_AHEAD))
        for jj in range(per_span):
            j = g * per_span + jj
            vb = vt_ref[0, :, j * tk:(j + 1) * tk]
            for i, s_span in enumerate(scores[g]):
                s = s_span[jj * tk:(jj + 1) * tk]
                m_new = jnp.maximum(m[i], jnp.max(s, axis=0, keepdims=True))
                p = jnp.exp2(s - m_new)
                alpha = jnp.exp2(m[i] - m_new)
                acc[i] = alpha * acc[i] + _mm(vb, p.astype(BF16))
                m[i] = m_new
        scores[g] = None
    for i in range(len(subs)):
        ot_ref[0, :, i * MXU_COLS:(i + 1) * MXU_COLS] = (acc[i][0:V_HEAD] / acc[i][V_HEAD:V_HEAD + 1]).astype(BF16)


def _attn(q, k, vt):
    B, H, S, _ = q.shape
    tq = TQ_ATTN
    return pl.pallas_call(
        _attn_kernel,
        grid=(B, H, S // tq),
        in_specs=[pl.BlockSpec((1, 1, tq, LANE), lambda b, h, i: (b, h, i, 0)),
                  pl.BlockSpec((1, 1, S, LANE), lambda b, h, i: (b, h, 0, 0)),
                  pl.BlockSpec((1, V_ROWS, S), lambda b, h, i: (b, h, 0))],
        out_specs=pl.BlockSpec((1, V_HEAD, tq), lambda b, h, i: (b, h, i)),
        out_shape=jax.ShapeDtypeStruct((B, H * V_HEAD, S), BF16),
        compiler_params=_params(("parallel", "parallel", "parallel")),
        name="attn",
    )(q, k, vt)


def _wkv_chain(r, k, v, kk, bv, lw, g, reverse):
    C = CHUNK
    sgn = -1 if reverse else 1
    rc = lax.broadcasted_iota(jnp.int32, (C, C), 0)
    cc = lax.broadcasted_iota(jnp.int32, (C, C), 1)
    tri = jnp.where((rc - cc) * sgn >= 0, 1.0, 0.0).astype(BF16)
    lw_hi, lw_lo = _split(lw)
    cum = _mm(tri, lw_hi) + _mm(tri, lw_lo)
    total = jnp.sum(lw, axis=0, keepdims=True)
    e_neg = jnp.exp(-cum)
    e_end = jnp.exp(total - cum)

    row = lax.broadcasted_iota(jnp.int32, (C, GW), 0)
    lane = lax.broadcasted_iota(jnp.int32, (C, GW), 1)
    lane_head = lane // RWKV_HEAD
    ahead = (row - lane % RWKV_HEAD) * sgn
    strict = ahead > 0
    incl = ahead >= 0

    def diag(x):
        xb = x.astype(BF16)
        return jnp.concatenate([jnp.where(lane_head == h, xb, jnp.zeros_like(xb)) for h in range(GROUP)], axis=0)

    def undiag(x):
        return sum(jnp.where(lane_head == h, x[h * C:(h + 1) * C], 0.0) for h in range(GROUP))

    a_t = (-kk * jnp.exp(cum - lw)).astype(BF16)
    r_t = r * jnp.exp(cum)
    dg_v = diag(v)

    yield
    pm = _mm_nt(jnp.concatenate([a_t, r_t.astype(BF16)], axis=0),
                jnp.concatenate([diag(bv * e_neg), diag(k * e_neg)], axis=0))
    n_pow = jnp.where(strict, pm[0:C, 0:GW], 0.0)
    m_ak = jnp.where(strict, pm[0:C, GW:2 * GW], 0.0)
    m_rb = jnp.where(incl, pm[C:2 * C, 0:GW], 0.0).astype(BF16)
    m_rk = jnp.where(incl, pm[C:2 * C, GW:2 * GW], 0.0)
    yield
    wo = _mm(jnp.concatenate([m_ak, m_rk], axis=0).astype(BF16), dg_v)
    w1, o0 = wo[0:C], wo[C:2 * C]

    levels = int(math.log2(C))
    tinv = jnp.where(ahead == 0, 1.0, n_pow)
    for lvl in range(levels):
        yield
        lhs = [] if lvl == levels - 1 else [n_pow]
        lhs += [] if lvl == 0 else [tinv]
        res = _mm(jnp.concatenate(lhs, axis=0).astype(BF16), diag(n_pow))
        if lvl > 0:
            tinv = tinv + res[(len(lhs) - 1) * C:len(lhs) * C]
        n_pow = res[0:C]

    yield
    au = _mm(tinv.astype(BF16), jnp.concatenate([diag(a_t), diag(w1)], axis=1))
    a2, u0 = au[:, 0:GW], au[:, GW:2 * GW]
    yield
    ro = _mm(m_rb, jnp.concatenate([diag(a2), diag(u0)], axis=1))
    r2 = r_t + ro[:, 0:GW]
    o0 = o0 + ro[:, GW:2 * GW]
    yield
    ends = jnp.concatenate([bv * e_end, k * e_end], axis=0).astype(BF16)
    vals = jnp.concatenate([jnp.concatenate([a2, u0], axis=1),
                            jnp.concatenate([jnp.zeros_like(v), v], axis=1)], axis=0).astype(BF16)
    pp = _mm_tn(ends, vals)
    phi, psi = undiag(pp[:, 0:GW]), undiag(pp[:, GW:2 * GW])
    yield
    res = _mm(jnp.concatenate([r2, phi], axis=0).astype(BF16), diag(g))
    on_diag = jnp.where(ahead == 0, jnp.exp(total), 0.0)
    p_end = sum(jnp.where(lane_head == h, jnp.sum(jnp.where(lane_head == h, on_diag, 0.0), axis=1, keepdims=True), 0.0)
                for h in range(GROUP))
    yield res[0:C] + o0, p_end * g + res[C:2 * C] + psi


def _wkv_kernel(fwd_ref, bwd_ref, lwf_ref, lwb_ref, of_ref, ob_ref, g_ref):
    @pl.when(pl.program_id(1) == 0)
    def _():
        g_ref[...] = jnp.zeros_like(g_ref)

    chains = []
    for bi in range(of_ref.shape[0]):
        for d, (in_ref, lw_ref, o_ref) in enumerate(((fwd_ref, lwf_ref, of_ref), (bwd_ref, lwb_ref, ob_ref))):
            for gidx in range(RWKV_WIDTH // GW):
                sl = slice(gidx * GW, (gidx + 1) * GW)
                ins = [in_ref[bi, :, seg.start + sl.start:seg.start + sl.stop].astype(F32)
                       for seg in (SCAN_R, SCAN_K, SCAN_V, SCAN_KK, SCAN_BV)]
                ins.append(lw_ref[bi, :, SCAN_LW[d].start + sl.start:SCAN_LW[d].start + sl.stop])
                gen = _wkv_chain(*ins, g_ref[bi, d, gidx], reverse=d == 1)
                chains.append((gen, o_ref, bi, sl, d, gidx))
    while chains:
        for chain in list(chains):
            gen, o_ref, bi, sl, d, gidx = chain
            out = next(gen)
            if out is not None:
                o_ref[bi, :, sl] = out[0].astype(o_ref.dtype)
                g_ref[bi, d, gidx] = out[1]
                chains.remove(chain)


def _wkv(scan, lw, B, S):
    T = scan.shape[0]
    W = RWKV_WIDTH
    C = CHUNK
    nb = NB_WKV
    nc = S // C
    blk = lambda w, rev: pl.BlockSpec((nb, C, w), lambda b, c: (b, nc - 1 - c if rev else c, 0))
    out = jax.ShapeDtypeStruct((B, S, W), BF16)
    scan = scan.reshape(B, S, SCAN_W)
    lw = lw.reshape(B, S, LW_W)
    o_f, o_b = pl.pallas_call(
        _wkv_kernel,
        grid=(B // nb, nc),
        in_specs=[blk(SCAN_W, False), blk(SCAN_W, True), blk(LW_W, False), blk(LW_W, True)],
        out_specs=[blk(W, False), blk(W, True)],
        out_shape=[out, out],
        scratch_shapes=[pltpu.VMEM((nb, 2, W // GW, RWKV_HEAD, GW), F32)],
        compiler_params=_params(("parallel", "arbitrary")),
        name="wkv",
    )(scan, scan, lw, lw)
    return o_f.reshape(T, W), o_b.reshape(T, W)


def _mix_kernel(of_ref, ob_ref, post_ref, oat_ref, gate_ref, x_ref, bo_ref, lng_ref, lnb_ref,
                woa_ref, wob_ref, wm_ref, out_ref):
    bo = bo_ref[...]
    bonus = post_ref[:, POST_BONUS].astype(F32)
    ob = None
    for d, o_ref in enumerate((of_ref, ob_ref)):
        o = o_ref[...].astype(F32)
        mu = _head_sums(o, bo) / RWKV_HEAD
        dl = o - mu
        var = _head_sums(dl * dl, bo) / RWKV_HEAD
        y = dl * lax.rsqrt(var + GN_EPS) * lng_ref[...] + lnb_ref[...] + bonus
        y = y * post_ref[:, POST_GATE[d]]
        ob = y if ob is None else ob + y
    y_b = _mm(ob.astype(BF16), wob_ref[...])
    y_a = _mm_tn(oat_ref[0], woa_ref[...])
    mixed = gate_ref[:, 0:D_MODEL] * y_a + gate_ref[:, D_MODEL:2 * D_MODEL] * y_b
    out_ref[...] = x_ref[...] + _mm(mixed.astype(BF16), wm_ref[...])


def _mix(o_f, o_b, post, oat, gates, x2, bo, lng, lnb, woa, wob, wm, B, S):
    T = x2.shape[0]
    tm = TM_MIX
    nj = S // tm
    consts = [bo, lng, lnb, woa, wob, wm]
    tok = lambda w: pl.BlockSpec((tm, w), lambda b, j: (b * nj + j, 0))
    return pl.pallas_call(
        _mix_kernel,
        grid=(B, nj),
        in_specs=[tok(RWKV_WIDTH), tok(RWKV_WIDTH), tok(POST_W),
                  pl.BlockSpec((1, MLA_HEADS * V_HEAD, tm), lambda b, j: (b, 0, j)),
                  pl.BlockSpec((tm, 2 * D_MODEL), lambda b, j: (b * nj + j, 0)),
                  pl.BlockSpec((tm, D_MODEL), lambda b, j: (b * nj + j, 0))]
        + [_const_spec(c.shape) for c in consts],
        out_specs=pl.BlockSpec((tm, D_MODEL), lambda b, j: (b * nj + j, 0)),
        out_shape=jax.ShapeDtypeStruct((T, D_MODEL), F32),
        compiler_params=_params(("parallel", "parallel")),
        name="mix",
    )(o_f, o_b, post, oat, gates, x2, *consts)


def _ffn_kernel(x_ref, before_ref, after_ref, g_ref, wg_ref, wu_ref, cw_ref, cb_ref, wd_ref, out_ref, *, tiles_per_seq):
    tm = x_ref.shape[0]
    x = x_ref[...]
    h_all = _rms(_with_neighbours(x, before_ref, after_ref, tiles_per_seq), g_ref[...]).astype(BF16)
    h = h_all[0:tm]
    chunks = [slice(c, c + TH_FFN) for c in range(0, FFN_HIDDEN, TH_FFN)]
    pres = [_mm(h_all, wg_ref[:, sl]) for sl in chunks]
    ups = [_mm(h, wu_ref[:, sl]) for sl in chunks]
    acc = x
    for sl, pre_all, up in zip(chunks, pres, ups):
        pre = pre_all[0:tm]
        prev, nxt = _shifted(pre, pre_all[tm:tm + 1, :], pre_all[tm + SUBLANE:tm + SUBLANE + 1, :])
        gp = prev * cw_ref[0:1, sl] + pre * cw_ref[1:2, sl] + nxt * cw_ref[2:3, sl] + cb_ref[:, sl]
        act = gp * _sigmoid(gp) * up
        acc = acc + _mm(act.astype(BF16), wd_ref[sl, :])
    out_ref[...] = acc


def _ffn(x1, g, wg, wu, cw, cb, wd, S):
    T = x1.shape[0]
    tm = TM_FFN
    consts = [g, wg, wu, cw, cb, wd]
    return pl.pallas_call(
        functools.partial(_ffn_kernel, tiles_per_seq=S // tm),
        grid=(T // tm,),
        in_specs=[pl.BlockSpec((tm, D_MODEL), lambda i: (i, 0))] + _halo_specs(tm, D_MODEL, T)
        + [_const_spec(c.shape) for c in consts],
        out_specs=pl.BlockSpec((tm, D_MODEL), lambda i: (i, 0)),
        out_shape=jax.ShapeDtypeStruct((T, D_MODEL), F32),
        compiler_params=_params(("parallel",)),
        name="ffn",
    )(x1, x1, x1, *consts)


def _pad_cols(w, width):
    return jnp.pad(w, ((0, 0), (0, width - w.shape[1])))


def _pad_rows(w, height):
    return jnp.pad(w, ((0, height - w.shape[0]), (0, 0)))


def _rw_layout(t):
    W = RWKV_WIDTH
    pad = lambda a: jnp.pad(a, [(0, 0)] * (a.ndim - 1) + [(0, LANE - a.shape[-1])])
    o = 3 * W
    return jnp.concatenate([t[..., 0:o], pad(t[..., o:o + 64]), pad(t[..., o + 64:o + 128]),
                            pad(t[..., o + 128:o + 192]), t[..., o + 192:o + 448]], axis=-1)


def kernel(x, positions, norm_mix_g, w_in, b_gate, q_a_norm_g, kv_a_norm_g, w_uq, w_ukv, qn_norm_g, qr_norm_g,
           kn_norm_g, kr_norm_g, shift_mu, w0, w2, a0, a2, g2, k_k, k_a, r_k, ln_x_g, ln_x_b, w_o, w_merge,
           norm_ffn_g, w_ffn_gate, w_ffn_up, ffn_conv_w, ffn_conv_b, w_ffn_down):
    B, S, D = x.shape
    T = B * S
    depth = norm_mix_g.shape[0]
    H = MLA_HEADS
    mla_cols = Q_LORA + KV_LORA + QK_ROPE
    rw_cols = 3 * RWKV_WIDTH + A_LORA + 2 * DECAY_LORA + 2 * GATE_LORA

    half = QK_ROPE // 2
    inv_freq = ROPE_THETA ** (-jnp.arange(0, QK_ROPE, 2, dtype=F32) / QK_ROPE)
    freq = inv_freq.reshape(half, 1)
    fr = jnp.arange(half)
    place = jnp.zeros((3, half, LANE), F32)
    place = place.at[0, fr, QK_NOPE + fr].set(1.0).at[0, fr, QK_NOPE + half + fr].set(1.0)
    place = place.at[1, fr, QK_NOPE + fr].set(-1.0)
    place = place.at[2, fr, QK_NOPE + half + fr].set(1.0)
    place = place.astype(BF16)
    lane = jnp.arange(LANE)
    base = ((lane < QK_NOPE) | (lane >= QK_NOPE + QK_ROPE)).astype(F32).reshape(1, LANE)
    lane2 = jnp.arange(2 * LANE)
    seg_of = 2 * (lane2 // LANE) + (lane2 % LANE >= QK_NOPE)
    seg = (seg_of[:, None] == seg_of[None, :]).astype(BF16)
    invn = jnp.where(lane2 % LANE < QK_NOPE, 1.0 / QK_NOPE, 1.0 / QK_ROPE).astype(F32).reshape(1, 2 * LANE)
    head_of = jnp.arange(GW) // RWKV_HEAD
    bo = (head_of[:, None] == head_of[None, :]).astype(BF16)
    pos_row = positions.reshape(1, T)
    row = lambda v: v.reshape(1, -1).astype(F32)
    scale = math.log2(math.e) / math.sqrt(QK_NOPE + QK_ROPE)

    x2 = x.reshape(T, D)
    for l in range(depth):
        wi = w_in[l]
        w_mla = jnp.concatenate([wi[:, 0:Q_LORA + KV_LORA], jnp.zeros((D, QK_NOPE), F32),
                                 wi[:, Q_LORA + KV_LORA:mla_cols], jnp.zeros((D, LANE - QK_NOPE - QK_ROPE), F32)], axis=1)
        w_rw = _rw_layout(wi[:, mla_cols:mla_cols + rw_cols])
        w_gate = wi[:, mla_cols + rw_cols:]
        wq = _pad_cols(w_uq[l].reshape(Q_LORA, H, QK_NOPE + QK_ROPE).transpose(1, 0, 2).reshape(H * Q_LORA, -1), LANE)
        wq = wq.reshape(H // 2, 2, Q_LORA, LANE).transpose(0, 2, 1, 3).reshape(H // 2, Q_LORA, 2 * LANE).astype(BF16)
        wkv = w_ukv[l].reshape(KV_LORA, H, QK_NOPE + V_HEAD)
        wk = _pad_cols(wkv[:, :, :QK_NOPE].transpose(1, 0, 2).reshape(H * KV_LORA, QK_NOPE), LANE)
        wk = wk.reshape(H // 2, 2, KV_LORA, LANE).transpose(0, 2, 1, 3).reshape(H // 2, KV_LORA, 2 * LANE).astype(BF16)
        wvt = jnp.pad(wkv[:, :, QK_NOPE:], ((0, 0), (0, 0), (0, V_ROWS - V_HEAD))).reshape(KV_LORA, H * V_ROWS).T.astype(BF16)
        one = jnp.tile((jnp.arange(V_ROWS) == V_HEAD).astype(F32), H).reshape(H * V_ROWS, 1)
        gqh = jnp.tile(_pad_cols(jnp.concatenate([qn_norm_g[l], qr_norm_g[l]]).reshape(1, -1), LANE) * scale, (1, 2))
        gkn = jnp.tile(_pad_cols(kn_norm_g[l].reshape(1, -1), LANE), (1, 2))
        gkr = _pad_cols(jnp.concatenate([jnp.zeros((QK_NOPE,), F32), kr_norm_g[l]]).reshape(1, -1), LANE)
        mu = _rw_layout(shift_mu[l])
        mu = jnp.concatenate([mu, 1.0 - mu[0:1] - mu[1:2]], axis=0)
        a2p = _pad_rows(a2[l], LANE).astype(BF16)
        w2p = jnp.pad(w2[l], ((0, 0), (0, LANE - DECAY_LORA), (0, 0))).astype(BF16)
        consts = [row(norm_mix_g[l]), w_mla.astype(BF16), w_rw.astype(BF16), w_gate.astype(BF16),
                  b_gate[l].reshape(1, 2 * D),
                  row(q_a_norm_g[l]), row(kv_a_norm_g[l]), wq, wk, wvt, gqh, gkn, gkr, freq, place, base, one, seg, invn,
                  mu, row(a0[l]), a2p, w0[l], w2p, g2[l].astype(BF16), row(k_k[l]), row(k_a[l]), row(r_k[l]), bo]

        gates, q, k, vt, scan, lw, post = _front(x2, pos_row, consts, B, S)
        oat = _attn(q, k, vt)
        o_f, o_b = _wkv(scan, lw, B, S)
        wo = w_o[l].astype(BF16)
        x2 = _mix(o_f, o_b, post, oat, gates, x2, bo, row(ln_x_g[l]), row(ln_x_b[l]), wo[:H * V_HEAD],
                  wo[H * V_HEAD:], w_merge[l].astype(BF16), B, S)
        x2 = _ffn(x2, row(norm_ffn_g[l]), w_ffn_gate[l].astype(BF16), w_ffn_up[l].astype(BF16), ffn_conv_w[l],
                  ffn_conv_b[l].reshape(1, -1), w_ffn_down[l].astype(BF16), S)
    return x2.reshape(B, S, D)
```

```python
import functools
import math

import jax
import jax.numpy as jnp
from jax import lax
from jax.experimental import pallas as pl
from jax.experimental.pallas import tpu as pltpu

F32 = jnp.float32
BF16 = jnp.bfloat16

D_MODEL = 1024
MLA_HEADS = 8
QK_NOPE = 64
QK_ROPE = 32
V_HEAD = 64
Q_LORA = 256
KV_LORA = 128
ROPE_THETA = 10000.0
RWKV_HEADS = 8
RWKV_HEAD = 64
RWKV_WIDTH = RWKV_HEADS * RWKV_HEAD
DECAY_LORA = 64
A_LORA = 64
GATE_LORA = 128
GN_EPS = 64e-5
FFN_HIDDEN = 2816
NORM_EPS = 1e-6

LANE = 128
SUBLANE = 8
MXU_COLS = 256
VMEM_LIMIT = 56 * 1024 * 1024

V_ROWS = V_HEAD + 16
MLA_W = 512
RW_W = 3 * RWKV_WIDTH + 5 * LANE

CHUNK = 64
GROUP = 4
GW = GROUP * RWKV_HEAD
assert CHUNK == RWKV_HEAD

_seg = lambda i: slice(i * RWKV_WIDTH, (i + 1) * RWKV_WIDTH)
SCAN_R, SCAN_K, SCAN_V, SCAN_KK, SCAN_BV = (_seg(i) for i in range(5))
SCAN_LW = (_seg(5), _seg(6))
SCAN_W = 7 * RWKV_WIDTH
POST_BONUS = _seg(0)
POST_GATE = (_seg(1), _seg(2))
POST_W = 3 * RWKV_WIDTH

TM_FRONT = 512
GATE_CHUNK = MXU_COLS
TQ_ATTN = 2048
TK_ATTN = 256
QK_SPAN = 1024
ATTN_AHEAD = 1
NB_WKV = 2
TM_MIX = 1024
TM_FFN = 512
TH_FFN = FFN_HIDDEN


def _mm(a, b):
    return jnp.dot(a, b, preferred_element_type=F32)


def _mm_nt(a, b):
    return lax.dot_general(a, b, (((1,), (1,)), ((), ())), preferred_element_type=F32)


def _mm_tn(a, b):
    return lax.dot_general(a, b, (((0,), (0,)), ((), ())), preferred_element_type=F32)


def _split(x):
    hi = x.astype(BF16)
    lo = (x - hi.astype(F32)).astype(BF16)
    return hi, lo


def _head_sums(x, bo):
    xb = x.astype(BF16)
    return jnp.concatenate([_mm(xb[:, c:c + GW], bo) for c in range(0, x.shape[1], GW)], axis=1)


def _sigmoid(x):
    return 0.5 * jnp.tanh(0.5 * x) + 0.5


def _rms(x, g, eps=NORM_EPS):
    return x * lax.rsqrt(jnp.mean(x * x, axis=-1, keepdims=True) + eps) * g


def _const_spec(shape):
    nd = len(shape)
    return pl.BlockSpec(shape, lambda *_: (0,) * nd, pipeline_mode=pl.Buffered(1))


def _params(sem):
    return pltpu.CompilerParams(dimension_semantics=sem, vmem_limit_bytes=VMEM_LIMIT)


def _run_in_turn(streams):
    streams = list(streams)
    while streams:
        for st in list(streams):
            if next(st, "done") == "done":
                streams.remove(st)


def _shifted(p, before, after):
    tm = p.shape[0]
    row = lax.broadcasted_iota(jnp.int32, p.shape, 0)
    prev = jnp.where(row == 0, before, pltpu.roll(p, 1, axis=0))
    nxt = jnp.where(row == tm - 1, after, pltpu.roll(p, tm - 1, axis=0))
    return prev, nxt


def _halo_specs(tm, width, T):
    per = tm // SUBLANE
    before = pl.BlockSpec((SUBLANE, width), lambda i: (jnp.maximum(i * per - 1, 0), 0))
    after = pl.BlockSpec((SUBLANE, width), lambda i: (jnp.minimum((i + 1) * per, T // SUBLANE - 1), 0))
    return [before, after]


def _with_neighbours(x, before_ref, after_ref, tiles_per_seq):
    j = pl.program_id(0) % tiles_per_seq
    before = jnp.where(j == 0, 0.0, before_ref[SUBLANE - 1:SUBLANE, :])
    after = jnp.where(j == tiles_per_seq - 1, 0.0, after_ref[0:1, :])
    width = x.shape[1]
    return jnp.concatenate([x, jnp.broadcast_to(before, (SUBLANE, width)), jnp.broadcast_to(after, (SUBLANE, width))],
                           axis=0)


def _rotary(t, cos, sin_up, sin_dn):
    up = pltpu.roll(t, LANE - QK_ROPE // 2, axis=1)
    dn = pltpu.roll(t, QK_ROPE // 2, axis=1)
    return t * cos + up * sin_up + dn * sin_dn


def _front_kernel(x_ref, xb_ref, xa_ref, pos_ref, gmix_ref, wm_ref, wr_ref, wg_ref, bg_ref,
                  gq_ref, gkv_ref, wq_ref, wk_ref, wvt_ref, gqh_ref, gkn_ref, gkr_ref, freq_ref, place_ref, base_ref,
                  one_ref, seg_ref, invn_ref,
                  mu_ref, a0_ref, a2_ref, w0_ref, w2_ref, g2_ref, kk_ref, ka_ref, rk_ref, bo_ref,
                  gate_out, q_out, k_out, vt_out,
                  scan_out, post_out, *, tiles_per_seq):
    tm = x_ref.shape[0]
    gmix = gmix_ref[...]
    h_all = _rms(_with_neighbours(x_ref[...], xb_ref, xa_ref, tiles_per_seq), gmix).astype(BF16)
    h = h_all[0:tm]

    def gates():
        for c in range(0, 2 * D_MODEL, GATE_CHUNK):
            sl = slice(c, c + GATE_CHUNK)
            gate_out[:, sl] = _sigmoid(_mm(h, wg_ref[:, sl]) + bg_ref[:, sl]).astype(gate_out.dtype)
            yield

    def rwkv():
        p_all = _mm(h_all, wr_ref[...])
        yield
        p = p_all[0:tm]
        prev, nxt = _shifted(p, p_all[tm:tm + 1, :], p_all[tm + SUBLANE:tm + SUBLANE + 1, :])
        u = p * mu_ref[2:3, :] + prev * mu_ref[0:1, :] + nxt * mu_ref[1:2, :]
        W = RWKV_WIDTH
        r, k, v = u[:, 0:W], u[:, W:2 * W], u[:, 2 * W:3 * W]
        scan_out[:, SCAN_R] = r
        scan_out[:, SCAN_V] = v
        yield
        a = _sigmoid(a0_ref[...] + _mm(u[:, 3 * W:3 * W + LANE].astype(BF16), a2_ref[...]))
        bo = bo_ref[...]
        kkf = k * kk_ref[...]
        ss = _head_sums(kkf * kkf, bo)
        yield
        kkn = kkf * jnp.where(ss >= 1e-24, lax.rsqrt(ss), 1e12)
        k = k * (1.0 + (a - 1.0) * ka_ref[...])
        scan_out[:, SCAN_K] = k
        scan_out[:, SCAN_KK] = kkn
        scan_out[:, SCAN_BV] = kkn * a
        yield
        post_out[:, POST_BONUS] = (_head_sums(r * k * rk_ref[...], bo) * v).astype(post_out.dtype)
        yield
        for d in range(2):
            dlo = u[:, 3 * W + (1 + d) * LANE:3 * W + (2 + d) * LANE]
            glo = u[:, 3 * W + (3 + d) * LANE:3 * W + (4 + d) * LANE]
            z = -(w0_ref[d:d + 1, :] + _mm(jnp.tanh(dlo).astype(BF16), w2_ref[d]))
            softplus = jnp.maximum(z, 0.0) + jnp.log(1.0 + jnp.exp(-jnp.abs(z)))
            scan_out[:, SCAN_LW[d]] = -jnp.exp(-softplus - 0.5)
            yield
            post_out[:, POST_GATE[d]] = _mm(_sigmoid(glo).astype(BF16), g2_ref[d]).astype(post_out.dtype)
            yield

    def mla():
        lat = _mm(h, wm_ref[...])
        ang = freq_ref[...] * pos_ref[...].astype(F32)
        place = place_ref[...]
        cos_hl = jnp.concatenate(_split(jnp.cos(ang)), axis=0)
        sin_hl = jnp.concatenate(_split(jnp.sin(ang)), axis=0)
        yield
        cos = _mm_tn(cos_hl, jnp.concatenate([place[0], place[0]], axis=0)) + base_ref[...]
        sin_up = _mm_tn(sin_hl, jnp.concatenate([place[1], place[1]], axis=0))
        sin_dn = _mm_tn(sin_hl, jnp.concatenate([place[2], place[2]], axis=0))
        seg, invn = seg_ref[...], invn_ref[...]

        def head_norm(t, gain):
            ss = _mm((t * t).astype(BF16), seg[0:t.shape[1], 0:t.shape[1]])
            return t * lax.rsqrt(ss * invn[:, 0:t.shape[1]] + NORM_EPS) * gain

        cq = _rms(lat[:, 0:Q_LORA], gq_ref[...]).astype(BF16)
        ckv = _rms(lat[:, Q_LORA:Q_LORA + KV_LORA], gkv_ref[...]).astype(BF16)
        yield
        kr = _rotary(head_norm(lat[:, Q_LORA + KV_LORA:MLA_W], gkr_ref[...]), cos, sin_up, sin_dn)
        vt_out[0] = (_mm_nt(wvt_ref[...], ckv) + one_ref[...]).astype(BF16)
        yield
        for pair in range(MLA_HEADS // 2):
            q2 = head_norm(_mm(cq, wq_ref[pair]), gqh_ref[...])
            for i in range(2):
                q_out[0, 2 * pair + i] = _rotary(q2[:, i * LANE:(i + 1) * LANE], cos, sin_up, sin_dn).astype(BF16)
            yield
            k2 = head_norm(_mm(ckv, wk_ref[pair]), gkn_ref[...])
            for i in range(2):
                k_out[0, 2 * pair + i] = (k2[:, i * LANE:(i + 1) * LANE] + kr).astype(BF16)
            yield

    _run_in_turn([gates(), rwkv(), mla()])


def _front(x2, pos_row, consts, B, S):
    T = x2.shape[0]
    tm = TM_FRONT
    nj = S // tm
    H = MLA_HEADS
    tok = lambda w: pl.BlockSpec((tm, w), lambda i: (i, 0))
    heads = pl.BlockSpec((1, H, tm, LANE), lambda i: (i // nj, 0, i % nj, 0))
    return pl.pallas_call(
        functools.partial(_front_kernel, tiles_per_seq=nj),
        grid=(T // tm,),
        in_specs=[tok(D_MODEL)] + _halo_specs(tm, D_MODEL, T) + [pl.BlockSpec((1, tm), lambda i: (0, i))]
        + [_const_spec(c.shape) for c in consts],
        out_specs=[tok(2 * D_MODEL), heads, heads, pl.BlockSpec((1, H * V_ROWS, tm), lambda i: (i // nj, 0, i % nj))]
        + [tok(SCAN_W), tok(POST_W)],
        out_shape=[jax.ShapeDtypeStruct((T, 2 * D_MODEL), BF16), jax.ShapeDtypeStruct((B, H, S, LANE), BF16),
                   jax.ShapeDtypeStruct((B, H, S, LANE), BF16), jax.ShapeDtypeStruct((B, H * V_ROWS, S), BF16),
                   jax.ShapeDtypeStruct((T, SCAN_W), F32), jax.ShapeDtypeStruct((T, POST_W), BF16)],
        compiler_params=_params(("parallel",)),
        name="front",
    )(x2, x2, x2, pos_row, *consts)


def _attn_kernel(q_ref, k_ref, vt_ref, ot_ref):
    S = k_ref.shape[2]
    tk = TK_ATTN
    subs = [q_ref[0, 0, i:i + MXU_COLS, :] for i in range(0, q_ref.shape[2], MXU_COLS)]
    per_span = QK_SPAN // tk
    nspan = S // QK_SPAN

    def scores_for(g):
        kb = k_ref[0, 0, g * QK_SPAN:(g + 1) * QK_SPAN, :]
        return [_mm_nt(kb, q) for q in subs]

    m = [jnp.full((1, MXU_COLS), -jnp.inf, F32) for _ in subs]
    acc = [jnp.zeros((V_ROWS, MXU_COLS), F32) for _ in subs]
    scores = [scores_for(g) for g in range(min(ATTN_AHEAD, nspan))]
    for g in range(nspan):
        if g + ATTN_AHEAD < nspan:
            scores.append(scores_for(g + ATTN_AHEAD))
        for jj in range(per_span):
            j = g * per_span + jj
            vb = vt_ref[0, :, j * tk:(j + 1) * tk]
            for i, s_span in enumerate(scores[g]):
                s = s_span[jj * tk:(jj + 1) * tk]
                m_new = jnp.maximum(m[i], jnp.max(s, axis=0, keepdims=True))
                p = jnp.exp2(s - m_new)
                alpha = jnp.exp2(m[i] - m_new)
                acc[i] = alpha * acc[i] + _mm(vb, p.astype(BF16))
                m[i] = m_new
        scores[g] = None
    for i in range(len(subs)):
        ot_ref[0, :, i * MXU_COLS:(i + 1) * MXU_COLS] = (acc[i][0:V_HEAD] / acc[i][V_HEAD:V_HEAD + 1]).astype(BF16)


def _attn(q, k, vt):
    B, H, S, _ = q.shape
    tq = TQ_ATTN
    return pl.pallas_call(
        _attn_kernel,
        grid=(B, H, S // tq),
        in_specs=[pl.BlockSpec((1, 1, tq, LANE), lambda b, h, i: (b, h, i, 0)),
                  pl.BlockSpec((1, 1, S, LANE), lambda b, h, i: (b, h, 0, 0)),
                  pl.BlockSpec((1, V_ROWS, S), lambda b, h, i: (b, h, 0))],
        out_specs=pl.BlockSpec((1, V_HEAD, tq), lambda b, h, i: (b, h, i)),
        out_shape=jax.ShapeDtypeStruct((B, H * V_HEAD, S), BF16),
        compiler_params=_params(("parallel", "parallel", "parallel")),
        name="attn",
    )(q, k, vt)


def _wkv_chain(r, k, v, kk, bv, lw, g, reverse):
    C = CHUNK
    sgn = -1 if reverse else 1
    rc = lax.broadcasted_iota(jnp.int32, (C, C), 0)
    cc = lax.broadcasted_iota(jnp.int32, (C, C), 1)
    tri = jnp.where((rc - cc) * sgn >= 0, 1.0, 0.0).astype(BF16)
    lw_hi, lw_lo = _split(lw)
    cum = _mm(tri, lw_hi) + _mm(tri, lw_lo)
    total = jnp.sum(lw, axis=0, keepdims=True)
    e_neg = jnp.exp(-cum)
    e_end = jnp.exp(total - cum)

    row = lax.broadcasted_iota(jnp.int32, (C, GW), 0)
    lane = lax.broadcasted_iota(jnp.int32, (C, GW), 1)
    lane_head = lane // RWKV_HEAD
    ahead = (row - lane % RWKV_HEAD) * sgn
    strict = ahead > 0
    incl = ahead >= 0

    def diag(x):
        xb = x.astype(BF16)
        return jnp.concatenate([jnp.where(lane_head == h, xb, jnp.zeros_like(xb)) for h in range(GROUP)], axis=0)

    def undiag(x):
        return sum(jnp.where(lane_head == h, x[h * C:(h + 1) * C], 0.0) for h in range(GROUP))

    a_t = (-kk * jnp.exp(cum - lw)).astype(BF16)
    r_t = r * jnp.exp(cum)
    dg_v = diag(v)

    yield
    pm = _mm_nt(jnp.concatenate([a_t, r_t.astype(BF16)], axis=0),
                jnp.concatenate([diag(bv * e_neg), diag(k * e_neg)], axis=0))
    n_pow = jnp.where(strict, pm[0:C, 0:GW], 0.0)
    m_ak = jnp.where(strict, pm[0:C, GW:2 * GW], 0.0)
    m_rb = jnp.where(incl, pm[C:2 * C, 0:GW], 0.0).astype(BF16)
    m_rk = jnp.where(incl, pm[C:2 * C, GW:2 * GW], 0.0)
    yield
    wo = _mm(jnp.concatenate([m_ak, m_rk], axis=0).astype(BF16), dg_v)
    w1, o0 = wo[0:C], wo[C:2 * C]

    levels = int(math.log2(C))
    tinv = jnp.where(ahead == 0, 1.0, n_pow)
    for lvl in range(levels):
        yield
        lhs = [] if lvl == levels - 1 else [n_pow]
        lhs += [] if lvl == 0 else [tinv]
        res = _mm(jnp.concatenate(lhs, axis=0).astype(BF16), diag(n_pow))
        if lvl > 0:
            tinv = tinv + res[(len(lhs) - 1) * C:len(lhs) * C]
        n_pow = res[0:C]

    yield
    au = _mm(tinv.astype(BF16), jnp.concatenate([diag(a_t), diag(w1)], axis=1))
    a2, u0 = au[:, 0:GW], au[:, GW:2 * GW]
    yield
    ro = _mm(m_rb, jnp.concatenate([diag(a2), diag(u0)], axis=1))
    r2 = r_t + ro[:, 0:GW]
    o0 = o0 + ro[:, GW:2 * GW]
    yield
    ends = jnp.concatenate([bv * e_end, k * e_end], axis=0).astype(BF16)
    vals = jnp.concatenate([jnp.concatenate([a2, u0], axis=1),
                            jnp.concatenate([jnp.zeros_like(v), v], axis=1)], axis=0).astype(BF16)
    pp = _mm_tn(ends, vals)
    phi, psi = undiag(pp[:, 0:GW]), undiag(pp[:, GW:2 * GW])
    yield
    res = _mm(jnp.concatenate([r2, phi], axis=0).astype(BF16), diag(g))
    on_diag = jnp.where(ahead == 0, jnp.exp(total), 0.0)
    p_end = sum(jnp.where(lane_head == h, jnp.sum(jnp.where(lane_head == h, on_diag, 0.0), axis=1, keepdims=True), 0.0)
                for h in range(GROUP))
    yield res[0:C] + o0, p_end * g + res[C:2 * C] + psi


def _wkv_kernel(fwd_ref, bwd_ref, of_ref, ob_ref, g_ref):
    @pl.when(pl.program_id(1) == 0)
    def _():
        g_ref[...] = jnp.zeros_like(g_ref)

    chains = []
    for bi in range(of_ref.shape[0]):
        for d, (in_ref, o_ref) in enumerate(((fwd_ref, of_ref), (bwd_ref, ob_ref))):
            for gidx in range(RWKV_WIDTH // GW):
                sl = slice(gidx * GW, (gidx + 1) * GW)
                ins = [in_ref[bi, :, seg.start + sl.start:seg.start + sl.stop]
                       for seg in (SCAN_R, SCAN_K, SCAN_V, SCAN_KK, SCAN_BV, SCAN_LW[d])]
                gen = _wkv_chain(*ins, g_ref[bi, d, gidx], reverse=d == 1)
                chains.append((gen, o_ref, bi, sl, d, gidx))
    while chains:
        for chain in list(chains):
            gen, o_ref, bi, sl, d, gidx = chain
            out = next(gen)
            if out is not None:
                o_ref[bi, :, sl] = out[0].astype(o_ref.dtype)
                g_ref[bi, d, gidx] = out[1]
                chains.remove(chain)


def _wkv(scan, B, S):
    T = scan.shape[0]
    W = RWKV_WIDTH
    C = CHUNK
    nb = NB_WKV
    nc = S // C
    blk = lambda w, rev: pl.BlockSpec((nb, C, w), lambda b, c: (b, nc - 1 - c if rev else c, 0))
    out = jax.ShapeDtypeStruct((B, S, W), BF16)
    scan = scan.reshape(B, S, SCAN_W)
    o_f, o_b = pl.pallas_call(
        _wkv_kernel,
        grid=(B // nb, nc),
        in_specs=[blk(SCAN_W, False), blk(SCAN_W, True)],
        out_specs=[blk(W, False), blk(W, True)],
        out_shape=[out, out],
        scratch_shapes=[pltpu.VMEM((nb, 2, W // GW, RWKV_HEAD, GW), F32)],
        compiler_params=_params(("parallel", "arbitrary")),
        name="wkv",
    )(scan, scan)
    return o_f.reshape(T, W), o_b.reshape(T, W)


def _mix_kernel(of_ref, ob_ref, post_ref, oat_ref, gate_ref, x_ref, bo_ref, lng_ref, lnb_ref,
                woa_ref, wob_ref, wm_ref, out_ref):
    bo = bo_ref[...]
    bonus = post_ref[:, POST_BONUS].astype(F32)
    ob = None
    for d, o_ref in enumerate((of_ref, ob_ref)):
        o = o_ref[...].astype(F32)
        mu = _head_sums(o, bo) / RWKV_HEAD
        dl = o - mu
        var = _head_sums(dl * dl, bo) / RWKV_HEAD
        y = dl * lax.rsqrt(var + GN_EPS) * lng_ref[...] + lnb_ref[...] + bonus
        y = y * post_ref[:, POST_GATE[d]]
        ob = y if ob is None else ob + y
    y_b = _mm(ob.astype(BF16), wob_ref[...])
    y_a = _mm_tn(oat_ref[0], woa_ref[...])
    mixed = gate_ref[:, 0:D_MODEL] * y_a + gate_ref[:, D_MODEL:2 * D_MODEL] * y_b
    out_ref[...] = x_ref[...] + _mm(mixed.astype(BF16), wm_ref[...])


def _mix(o_f, o_b, post, oat, gates, x2, bo, lng, lnb, woa, wob, wm, B, S):
    T = x2.shape[0]
    tm = TM_MIX
    nj = S // tm
    consts = [bo, lng, lnb, woa, wob, wm]
    tok = lambda w: pl.BlockSpec((tm, w), lambda b, j: (b * nj + j, 0))
    return pl.pallas_call(
        _mix_kernel,
        grid=(B, nj),
        in_specs=[tok(RWKV_WIDTH), tok(RWKV_WIDTH), tok(POST_W),
                  pl.BlockSpec((1, MLA_HEADS * V_HEAD, tm), lambda b, j: (b, 0, j)),
                  pl.BlockSpec((tm, 2 * D_MODEL), lambda b, j: (b * nj + j, 0)),
                  pl.BlockSpec((tm, D_MODEL), lambda b, j: (b * nj + j, 0))]
        + [_const_spec(c.shape) for c in consts],
        out_specs=pl.BlockSpec((tm, D_MODEL), lambda b, j: (b * nj + j, 0)),
        out_shape=jax.ShapeDtypeStruct((T, D_MODEL), F32),
        compiler_params=_params(("parallel", "parallel")),
        name="mix",
    )(o_f, o_b, post, oat, gates, x2, *consts)


def _ffn_kernel(x_ref, before_ref, after_ref, g_ref, wg_ref, wu_ref, cw_ref, cb_ref, wd_ref, out_ref, *, tiles_per_seq):
    tm = x_ref.shape[0]
    x = x_ref[...]
    h_all = _rms(_with_neighbours(x, before_ref, after_ref, tiles_per_seq), g_ref[...]).astype(BF16)
    h = h_all[0:tm]
    chunks = [slice(c, c + TH_FFN) for c in range(0, FFN_HIDDEN, TH_FFN)]
    pres = [_mm(h_all, wg_ref[:, sl]) for sl in chunks]
    ups = [_mm(h, wu_ref[:, sl]) for sl in chunks]
    acc = x
    for sl, pre_all, up in zip(chunks, pres, ups):
        pre = pre_all[0:tm]
        prev, nxt = _shifted(pre, pre_all[tm:tm + 1, :], pre_all[tm + SUBLANE:tm + SUBLANE + 1, :])
        gp = prev * cw_ref[0:1, sl] + pre * cw_ref[1:2, sl] + nxt * cw_ref[2:3, sl] + cb_ref[:, sl]
        act = gp * _sigmoid(gp) * up
        acc = acc + _mm(act.astype(BF16), wd_ref[sl, :])
    out_ref[...] = acc


def _ffn(x1, g, wg, wu, cw, cb, wd, S):
    T = x1.shape[0]
    tm = TM_FFN
    consts = [g, wg, wu, cw, cb, wd]
    return pl.pallas_call(
        functools.partial(_ffn_kernel, tiles_per_seq=S // tm),
        grid=(T // tm,),
        in_specs=[pl.BlockSpec((tm, D_MODEL), lambda i: (i, 0))] + _halo_specs(tm, D_MODEL, T)
        + [_const_spec(c.shape) for c in consts],
        out_specs=pl.BlockSpec((tm, D_MODEL), lambda i: (i, 0)),
        out_shape=jax.ShapeDtypeStruct((T, D_MODEL), F32),
        compiler_params=_params(("parallel",)),
        name="ffn",
    )(x1, x1, x1, *consts)


def _pad_cols(w, width):
    return jnp.pad(w, ((0, 0), (0, width - w.shape[1])))


def _pad_rows(w, height):
    return jnp.pad(w, ((0, height - w.shape[0]), (0, 0)))


def _rw_layout(t):
    W = RWKV_WIDTH
    pad = lambda a: jnp.pad(a, [(0, 0)] * (a.ndim - 1) + [(0, LANE - a.shape[-1])])
    o = 3 * W
    return jnp.concatenate([t[..., 0:o], pad(t[..., o:o + 64]), pad(t[..., o + 64:o + 128]),
                            pad(t[..., o + 128:o + 192]), t[..., o + 192:o + 448]], axis=-1)


def kernel(x, positions, norm_mix_g, w_in, b_gate, q_a_norm_g, kv_a_norm_g, w_uq, w_ukv, qn_norm_g, qr_norm_g,
           kn_norm_g, kr_norm_g, shift_mu, w0, w2, a0, a2, g2, k_k, k_a, r_k, ln_x_g, ln_x_b, w_o, w_merge,
           norm_ffn_g, w_ffn_gate, w_ffn_up, ffn_conv_w, ffn_conv_b, w_ffn_down):
    B, S, D = x.shape
    T = B * S
    depth = norm_mix_g.shape[0]
    H = MLA_HEADS
    mla_cols = Q_LORA + KV_LORA + QK_ROPE
    rw_cols = 3 * RWKV_WIDTH + A_LORA + 2 * DECAY_LORA + 2 * GATE_LORA

    half = QK_ROPE // 2
    inv_freq = ROPE_THETA ** (-jnp.arange(0, QK_ROPE, 2, dtype=F32) / QK_ROPE)
    freq = inv_freq.reshape(half, 1)
    fr = jnp.arange(half)
    place = jnp.zeros((3, half, LANE), F32)
    place = place.at[0, fr, QK_NOPE + fr].set(1.0).at[0, fr, QK_NOPE + half + fr].set(1.0)
    place = place.at[1, fr, QK_NOPE + fr].set(-1.0)
    place = place.at[2, fr, QK_NOPE + half + fr].set(1.0)
    place = place.astype(BF16)
    lane = jnp.arange(LANE)
    base = ((lane < QK_NOPE) | (lane >= QK_NOPE + QK_ROPE)).astype(F32).reshape(1, LANE)
    lane2 = jnp.arange(2 * LANE)
    seg_of = 2 * (lane2 // LANE) + (lane2 % LANE >= QK_NOPE)
    seg = (seg_of[:, None] == seg_of[None, :]).astype(BF16)
    invn = jnp.where(lane2 % LANE < QK_NOPE, 1.0 / QK_NOPE, 1.0 / QK_ROPE).astype(F32).reshape(1, 2 * LANE)
    head_of = jnp.arange(GW) // RWKV_HEAD
    bo = (head_of[:, None] == head_of[None, :]).astype(BF16)
    pos_row = positions.reshape(1, T)
    row = lambda v: v.reshape(1, -1).astype(F32)
    scale = math.log2(math.e) / math.sqrt(QK_NOPE + QK_ROPE)

    x2 = x.reshape(T, D)
    for l in range(depth):
        wi = w_in[l]
        w_mla = jnp.concatenate([wi[:, 0:Q_LORA + KV_LORA], jnp.zeros((D, QK_NOPE), F32),
                                 wi[:, Q_LORA + KV_LORA:mla_cols], jnp.zeros((D, LANE - QK_NOPE - QK_ROPE), F32)], axis=1)
        w_rw = _rw_layout(wi[:, mla_cols:mla_cols + rw_cols])
        w_gate = wi[:, mla_cols + rw_cols:]
        wq = _pad_cols(w_uq[l].reshape(Q_LORA, H, QK_NOPE + QK_ROPE).transpose(1, 0, 2).reshape(H * Q_LORA, -1), LANE)
        wq = wq.reshape(H // 2, 2, Q_LORA, LANE).transpose(0, 2, 1, 3).reshape(H // 2, Q_LORA, 2 * LANE).astype(BF16)
        wkv = w_ukv[l].reshape(KV_LORA, H, QK_NOPE + V_HEAD)
        wk = _pad_cols(wkv[:, :, :QK_NOPE].transpose(1, 0, 2).reshape(H * KV_LORA, QK_NOPE), LANE)
        wk = wk.reshape(H // 2, 2, KV_LORA, LANE).transpose(0, 2, 1, 3).reshape(H // 2, KV_LORA, 2 * LANE).astype(BF16)
        wvt = jnp.pad(wkv[:, :, QK_NOPE:], ((0, 0), (0, 0), (0, V_ROWS - V_HEAD))).reshape(KV_LORA, H * V_ROWS).T.astype(BF16)
        one = jnp.tile((jnp.arange(V_ROWS) == V_HEAD).astype(F32), H).reshape(H * V_ROWS, 1)
        gqh = jnp.tile(_pad_cols(jnp.concatenate([qn_norm_g[l], qr_norm_g[l]]).reshape(1, -1), LANE) * scale, (1, 2))
        gkn = jnp.tile(_pad_cols(kn_norm_g[l].reshape(1, -1), LANE), (1, 2))
        gkr = _pad_cols(jnp.concatenate([jnp.zeros((QK_NOPE,), F32), kr_norm_g[l]]).reshape(1, -1), LANE)
        mu = _rw_layout(shift_mu[l])
        mu = jnp.concatenate([mu, 1.0 - mu[0:1] - mu[1:2]], axis=0)
        a2p = _pad_rows(a2[l], LANE).astype(BF16)
        w2p = jnp.pad(w2[l], ((0, 0), (0, LANE - DECAY_LORA), (0, 0))).astype(BF16)
        consts = [row(norm_mix_g[l]), w_mla.astype(BF16), w_rw.astype(BF16), w_gate.astype(BF16),
                  b_gate[l].reshape(1, 2 * D),
                  row(q_a_norm_g[l]), row(kv_a_norm_g[l]), wq, wk, wvt, gqh, gkn, gkr, freq, place, base, one, seg, invn,
                  mu, row(a0[l]), a2p, w0[l], w2p, g2[l].astype(BF16), row(k_k[l]), row(k_a[l]), row(r_k[l]), bo]

        gates, q, k, vt, scan, post = _front(x2, pos_row, consts, B, S)
        oat = _attn(q, k, vt)
        o_f, o_b = _wkv(scan, B, S)
        wo = w_o[l].astype(BF16)
        x2 = _mix(o_f, o_b, post, oat, gates, x2, bo, row(ln_x_g[l]), row(ln_x_b[l]), wo[:H * V_HEAD],
                  wo[H * V_HEAD:], w_merge[l].astype(BF16), B, S)
        x2 = _ffn(x2, row(norm_ffn_g[l]), w_ffn_gate[l].astype(BF16), w_ffn_up[l].astype(BF16), ffn_conv_w[l],
                  ffn_conv_b[l].reshape(1, -1), w_ffn_down[l].astype(BF16), S)
    return x2.reshape(B, S, D)
```
